```python
import math
import jax
import jax.numpy as jnp
from jax import lax
import numpy as np

D_MODEL = 1024
BATCH = 4
SEQ = 4096
DEPTH = 2

GRID_W = 64
CTX_LEN = 256
N_MIXERS = 2
MIXER_MLA = 0
MIXER_CONV = 1
N_A = (DEPTH + 1) // 2
N_B = DEPTH // 2
MLA_HEADS = 8
Q_LORA = 384
KV_LORA = 128
NOPE_DIM = 128
ROPE_DIM = 64
V_DIM = 128
QK_DIM = NOPE_DIM + ROPE_DIM
ROPE_THETA = 10000.0
Q_BLOCK = 128
CONV_WIDTH = 31
FFN_DIM = 2816
FFN_CONV_WIDTH = 3
EPS = 1e-6

kernel_name = "hybrid_mla_conformer_dit_block"


def rmsnorm(x, g):
    xf = x.astype(jnp.float32)
    y = xf * lax.rsqrt(jnp.mean(xf * xf, axis=-1, keepdims=True) + EPS)
    return y.astype(x.dtype) * g


def layernorm(x, g, b):
    xf = x.astype(jnp.float32)
    mu = jnp.mean(xf, axis=-1, keepdims=True)
    var = jnp.mean(jnp.square(xf - mu), axis=-1, keepdims=True)
    y = (xf - mu) * lax.rsqrt(var + EPS)
    return y.astype(x.dtype) * g + b


def modulate(h, shift, scale):
    return h * (1 + scale) + shift


def depthwise_conv(x, w):
    k, ch = w.shape
    pad = (k - 1) // 2
    return lax.conv_general_dilated(
        x, w[:, None, :].astype(x.dtype), window_strides=(1,), padding=[(pad, pad)],
        dimension_numbers=("NWC", "WIO", "NWC"), feature_group_count=ch)


def rope_tables(seq_len):
    rows = seq_len // GRID_W
    row = jnp.repeat(jnp.arange(rows, dtype=jnp.float32), GRID_W)
    col = jnp.tile(jnp.arange(GRID_W, dtype=jnp.float32), rows)
    n = ROPE_DIM // 4
    inv = ROPE_THETA ** (-jnp.arange(n, dtype=jnp.float32) / n)
    ang_r = row[:, None] * inv
    ang_c = col[:, None] * inv
    ang = jnp.concatenate([ang_r, ang_r, ang_c, ang_c], axis=-1)
    return jnp.cos(ang), jnp.sin(ang)


def apply_rope_2d(x, cos, sin):
    xf = x.astype(jnp.float32)
    xs = xf.reshape(xf.shape[:-1] + (2, 2, ROPE_DIM // 4))
    rot = jnp.stack([-xs[..., 1, :], xs[..., 0, :]], axis=-2).reshape(xf.shape)
    out = xf * cos[None, :, None, :] + rot * sin[None, :, None, :]
    return out.astype(x.dtype)


def mla_project(h, w_dqkv, g_q, w_uq, g_kv, w_ukv, cos, sin, with_q):
    b, l, _ = h.shape
    if with_q:
        d = h @ w_dqkv
        cq, dkv = d[..., :Q_LORA], d[..., Q_LORA:]
    else:
        dkv = h @ w_dqkv[:, Q_LORA:]
    ckv, kr = dkv[..., :KV_LORA], dkv[..., KV_LORA:]
    kv = (rmsnorm(ckv, g_kv) @ w_ukv).reshape(b, l, MLA_HEADS, NOPE_DIM + V_DIM)
    k_nope, v = kv[..., :NOPE_DIM], kv[..., NOPE_DIM:]
    kr = kr[:, :, None, :]
    if cos is not None:
        kr = apply_rope_2d(kr, cos, sin)
    k = jnp.concatenate([k_nope, jnp.broadcast_to(kr, (b, l, MLA_HEADS, ROPE_DIM))], axis=-1)
    q = None
    if with_q:
        q = (rmsnorm(cq, g_q) @ w_uq).reshape(b, l, MLA_HEADS, QK_DIM)
        q_nope, q_rope = q[..., :NOPE_DIM], q[..., NOPE_DIM:]
        if cos is not None:
            q_rope = apply_rope_2d(q_rope, cos, sin)
        q = jnp.concatenate([q_nope, q_rope], axis=-1)
    return q, k, v


def mla_mixer(h_lat, h_ctx, w_dqkv, g_q, w_uq, g_kv, w_ukv, w_o, cos, sin, ctx_out):
    b, s, _ = h_lat.shape
    scale = 1.0 / math.sqrt(QK_DIM)
    q_lat, k_lat, v_lat = mla_project(h_lat, w_dqkv, g_q, w_uq, g_kv, w_ukv, cos, sin, True)
    q_ctx, k_ctx, v_ctx = mla_project(h_ctx, w_dqkv, g_q, w_uq, g_kv, w_ukv, None, None, ctx_out)
    k_all = jnp.concatenate([k_ctx, k_lat], axis=1).astype(jnp.float32)
    v_all = jnp.concatenate([v_ctx, v_lat], axis=1).astype(jnp.float32)
    n_blk = s // Q_BLOCK
    qb = q_lat.reshape(b, n_blk, Q_BLOCK, MLA_HEADS, QK_DIM).transpose(1, 0, 2, 3, 4)

    def attend_block(q_blk):
        sc = jnp.einsum("bqhd,bkhd->bhqk", q_blk.astype(jnp.float32), k_all) * scale
        p = jax.nn.softmax(sc, axis=-1)
        return jnp.einsum("bhqk,bkhd->bqhd", p, v_all).astype(q_blk.dtype)

    o = lax.map(attend_block, qb)
    o = o.transpose(1, 0, 2, 3, 4).reshape(b, s, MLA_HEADS * V_DIM)
    y_lat = o @ w_o
    y_ctx = None
    if ctx_out:
        sc = jnp.einsum("bqhd,bkhd->bhqk", q_ctx.astype(jnp.float32),
                        k_ctx.astype(jnp.float32)) * scale
        p = jax.nn.softmax(sc, axis=-1)
        oc = jnp.einsum("bhqk,bkhd->bqhd", p, v_ctx.astype(jnp.float32)).astype(h_ctx.dtype)
        y_ctx = oc.reshape(oc.shape[0], oc.shape[1], MLA_HEADS * V_DIM) @ w_o
    return y_lat, y_ctx


def conformer_conv(h, w_pw1, b_pw1, w_dw, b_dw, g_ln, b_ln, w_pw2, b_pw2):
    u = h @ w_pw1 + b_pw1
    a, g = jnp.split(u, 2, axis=-1)
    u = a * jax.nn.sigmoid(g)
    u = depthwise_conv(u, w_dw) + b_dw
    u = jax.nn.silu(layernorm(u, g_ln, b_ln))
    return u @ w_pw2 + b_pw2


def conv_ffn(h, w_up, w_dw, b_dw, w_down):
    u = depthwise_conv(h @ w_up, w_dw) + b_dw
    gate, val = jnp.split(u, 2, axis=-1)
    return (jax.nn.silu(gate) * val) @ w_down


def setup_inputs(seed: int = 0) -> dict:
    key = jax.random.key(seed)
    ks = iter(jax.random.split(key, 40))
    f32 = jnp.float32
    D = D_MODEL

    def nrm(shape, scale):
        return jax.random.normal(next(ks), shape, f32) * scale

    def gain(shape):
        return 1.0 + nrm(shape, 0.05)

    return {
        "x": nrm((BATCH, SEQ, D), 1.0),
        "c": nrm((BATCH, D), 1.0),
        "ctx": nrm((BATCH, CTX_LEN, D), 1.0),
        "c_ctx": nrm((D,), 1.0),
        "w_mod": nrm((DEPTH, D, 6 * D), 0.5 * D ** -0.5),
        "b_mod": nrm((DEPTH, 6 * D), 0.01),
        "g_pre_mix": gain((DEPTH, D)),
        "g_post_mix": gain((DEPTH, D)),
        "g_pre_ffn": gain((DEPTH, D)),
        "g_post_ffn": gain((DEPTH, D)),
        "mla_w_dqkv": nrm((N_A, D, Q_LORA + KV_LORA + ROPE_DIM), D ** -0.5),
        "mla_g_q": gain((N_A, Q_LORA)),
        "mla_w_uq": nrm((N_A, Q_LORA, MLA_HEADS * QK_DIM), Q_LORA ** -0.5),
        "mla_g_kv": gain((N_A, KV_LORA)),
        "mla_w_ukv": nrm((N_A, KV_LORA, MLA_HEADS * (NOPE_DIM + V_DIM)), KV_LORA ** -0.5),
        "mla_w_o": nrm((N_A, MLA_HEADS * V_DIM, D), (MLA_HEADS * V_DIM) ** -0.5),
        "cv_w_pw1": nrm((N_B, D, 2 * D), D ** -0.5),
        "cv_b_pw1": nrm((N_B, 2 * D), 0.01),
        "cv_w_dw": nrm((N_B, CONV_WIDTH, D), CONV_WIDTH ** -0.5),
        "cv_b_dw": nrm((N_B, D), 0.01),
        "cv_g_ln": gain((N_B, D)),
        "cv_b_ln": nrm((N_B, D), 0.01),
        "cv_w_pw2": nrm((N_B, D, D), D ** -0.5),
        "cv_b_pw2": nrm((N_B, D), 0.01),
        "ffn_w_up": nrm((DEPTH, D, 2 * FFN_DIM), D ** -0.5),
        "ffn_w_dw": nrm((DEPTH, FFN_CONV_WIDTH, 2 * FFN_DIM), FFN_CONV_WIDTH ** -0.5),
        "ffn_b_dw": nrm((DEPTH, 2 * FFN_DIM), 0.01),
        "ffn_w_down": nrm((DEPTH, FFN_DIM, D), FFN_DIM ** -0.5),
    }


def reference(x, c, ctx, c_ctx, w_mod, b_mod, g_pre_mix, g_post_mix, g_pre_ffn, g_post_ffn,
              mla_w_dqkv, mla_g_q, mla_w_uq, mla_g_kv, mla_w_ukv, mla_w_o,
              cv_w_pw1, cv_b_pw1, cv_w_dw, cv_b_dw, cv_g_ln, cv_b_ln, cv_w_pw2, cv_b_pw2,
              ffn_w_up, ffn_w_dw, ffn_b_dw, ffn_w_down):
    seq_len = x.shape[1]
    cos, sin = rope_tables(seq_len)
    x_lat, x_ctx = x, ctx
    for i in range(DEPTH):
        kind = i % N_MIXERS
        j = i // N_MIXERS
        ctx_out = any(l % N_MIXERS == MIXER_MLA for l in range(i + 1, DEPTH))
        need_ctx = ctx_out or kind == MIXER_MLA

        m_lat = jnp.split((jax.nn.silu(c) @ w_mod[i] + b_mod[i])[:, None, :], 6, axis=-1)
        h_lat = modulate(rmsnorm(x_lat, g_pre_mix[i]), m_lat[0], m_lat[1])
        h_ctx, m_ctx = None, None
        if need_ctx:
            m_ctx = jnp.split(jax.nn.silu(c_ctx) @ w_mod[i] + b_mod[i], 6, axis=-1)
            h_ctx = modulate(rmsnorm(x_ctx, g_pre_mix[i]), m_ctx[0], m_ctx[1])

        if kind == MIXER_MLA:
            y_lat, y_ctx = mla_mixer(h_lat, h_ctx, mla_w_dqkv[j], mla_g_q[j], mla_w_uq[j],
                                     mla_g_kv[j], mla_w_ukv[j], mla_w_o[j], cos, sin, ctx_out)
        else:
            conv_args = (cv_w_pw1[j], cv_b_pw1[j], cv_w_dw[j], cv_b_dw[j], cv_g_ln[j],
                         cv_b_ln[j], cv_w_pw2[j], cv_b_pw2[j])
            y_lat = conformer_conv(h_lat, *conv_args)
            y_ctx = conformer_conv(h_ctx, *conv_args) if ctx_out else None

        x_lat = x_lat + m_lat[2] * rmsnorm(y_lat, g_post_mix[i])
        h = modulate(rmsnorm(x_lat, g_pre_ffn[i]), m_lat[3], m_lat[4])
        x_lat = x_lat + m_lat[5] * rmsnorm(
            conv_ffn(h, ffn_w_up[i], ffn_w_dw[i], ffn_b_dw[i], ffn_w_down[i]), g_post_ffn[i])

        if ctx_out:
            x_ctx = x_ctx + m_ctx[2] * rmsnorm(y_ctx, g_post_mix[i])
            hc = modulate(rmsnorm(x_ctx, g_pre_ffn[i]), m_ctx[3], m_ctx[4])
            x_ctx = x_ctx + m_ctx[5] * rmsnorm(
                conv_ffn(hc, ffn_w_up[i], ffn_w_dw[i], ffn_b_dw[i], ffn_w_down[i]), g_post_ffn[i])
    return x_lat
```

```python
import functools
import math

import jax
import jax.numpy as jnp
import numpy as np
from jax import lax
from jax.experimental import pallas as pl
from jax.experimental.pallas import tpu as pltpu

F32 = jnp.float32
BF16 = jnp.bfloat16

D_MODEL = 1024
DEPTH = 2
GRID_W = 64
MLA_HEADS = 8
Q_LORA = 384
KV_LORA = 128
NOPE_DIM = 128
ROPE_DIM = 64
V_DIM = 128
QK_DIM = NOPE_DIM + ROPE_DIM
ROPE_THETA = 10000.0
CONV_WIDTH = 31
FFN_DIM = 2816
FFN_CONV_WIDTH = 3
EPS = 1e-6

LANES = 128
QK_PAD = 2 * LANES
MOD_ROWS = 8
HALO = 16
VMEM_LIMIT = 56 * 1024 * 1024

TM_PROJ = 512
TQ_ATTN = 512
TK_ATTN = 512
TM_POST = 512
TM_FFN = 512
FFN_CHUNK = 256
TM_CONF = 512
CONF_ROWS = 128


def _rms(x):
    return x * lax.rsqrt(jnp.mean(x * x, axis=-1, keepdims=True) + EPS)


def _modulated_norm(x, g, shift, scale):
    return (_rms(x) * g) * (1.0 + scale) + shift


def _cparams(n_axes):
    return pltpu.CompilerParams(
        dimension_semantics=("parallel",) * n_axes, vmem_limit_bytes=VMEM_LIMIT)


def _resident(shape):
    zeros = (0,) * len(shape)
    return pl.BlockSpec(shape, lambda *_: zeros, pipeline_mode=pl.Buffered(1))


def _mod_kernel(cc_ref, w_ref, b_ref, o_ref):
    a = cc_ref[...]
    a = a * jax.nn.sigmoid(a)
    o_ref[...] = jnp.dot(a.astype(BF16), w_ref[...].astype(BF16),
                         preferred_element_type=F32) + b_ref[...]


def _adaln_mod(cc, w_mod, b_mod):
    d = D_MODEL
    return pl.pallas_call(
        _mod_kernel,
        out_shape=jax.ShapeDtypeStruct((DEPTH, 6, MOD_ROWS, d), F32),
        grid=(DEPTH, 6),
        in_specs=[
            pl.BlockSpec((MOD_ROWS, d), lambda i, j: (0, 0)),
            pl.BlockSpec((None, d, d), lambda i, j: (i, 0, j)),
            pl.BlockSpec((None, 1, d), lambda i, j: (i, 0, j)),
        ],
        out_specs=pl.BlockSpec((None, None, MOD_ROWS, d), lambda i, j: (i, j, 0, 0)),
        compiler_params=_cparams(2),
        name="adaln_mod",
    )(cc, w_mod, b_mod.reshape(DEPTH, 1, 6 * d))


def _mod_spec(layer, chunk):
    return pl.BlockSpec((None, None, MOD_ROWS, D_MODEL), lambda *_: (layer, chunk, 0, 0))


def _kv_project(h, wd_kv, g_kv, w_ukv):
    dkv = jnp.dot(h, wd_kv, preferred_element_type=F32)
    nkv = (_rms(dkv[:, :KV_LORA]) * g_kv).astype(BF16)
    kv = jnp.dot(nkv, w_ukv, preferred_element_type=F32)
    return kv, dkv[:, KV_LORA:]


def _store_kv(kv, kr, k_ref, v_ref):
    for hh in range(MLA_HEADS):
        base = hh * (NOPE_DIM + V_DIM)
        k_ref[hh, :, 0:NOPE_DIM] = kv[:, base:base + NOPE_DIM].astype(BF16)
        k_ref[hh, :, NOPE_DIM:QK_PAD] = kr
        v_ref[hh] = kv[:, base + NOPE_DIM:base + NOPE_DIM + V_DIM].astype(BF16)


def _mla_lat_kernel(x_ref, sh_ref, sc_ref, g_ref, wd_ref, gq_ref, wq_ref, gkv_ref, wukv_ref,
                    cos_ref, sin_ref, q_ref, k_ref, v_ref):
    b = pl.program_id(0)
    shift = sh_ref[pl.ds(b, 1), :]
    scale = sc_ref[pl.ds(b, 1), :]
    h = _modulated_norm(x_ref[...], g_ref[...], shift, scale).astype(BF16)
    cos = cos_ref[...]
    sin = sin_ref[...]

    kv, kr_raw = _kv_project(h, wd_ref[:, Q_LORA:], gkv_ref[...], wukv_ref[...])
    kr = (kr_raw[:, :LANES] * cos + kr_raw[:, LANES:] * sin).astype(BF16)
    _store_kv(kv, kr, k_ref, v_ref)

    cq = jnp.dot(h, wd_ref[:, :Q_LORA], preferred_element_type=F32)
    nq = (_rms(cq) * (gq_ref[...] * (1.0 / math.sqrt(QK_DIM)))).astype(BF16)
    n_nope = MLA_HEADS * NOPE_DIM
    n_rope = MLA_HEADS * LANES
    q_nope = jnp.dot(nq, wq_ref[:, :n_nope], preferred_element_type=F32)
    q_rope = jnp.dot(nq, wq_ref[:, n_nope:n_nope + n_rope], preferred_element_type=F32)
    q_swap = jnp.dot(nq, wq_ref[:, n_nope + n_rope:], preferred_element_type=F32)
    for hh in range(MLA_HEADS):
        q_ref[hh, :, 0:NOPE_DIM] = q_nope[:, hh * NOPE_DIM:(hh + 1) * NOPE_DIM].astype(BF16)
        cols = slice(hh * LANES, (hh + 1) * LANES)
        q_ref[hh, :, NOPE_DIM:QK_PAD] = (q_rope[:, cols] * cos + q_swap[:, cols] * sin).astype(BF16)


def _mla_ctx_kernel(x_ref, sh_ref, sc_ref, g_ref, wd_ref, gkv_ref, wukv_ref, k_ref, v_ref, *, row):
    shift = sh_ref[row:row + 1, :]
    scale = sc_ref[row:row + 1, :]
    h = _modulated_norm(x_ref[...], g_ref[...], shift, scale).astype(BF16)
    kv, kr_raw = _kv_project(h, wd_ref[...], gkv_ref[...], wukv_ref[...])
    _store_kv(kv, kr_raw[:, :LANES].astype(BF16), k_ref, v_ref)


def _mla_project_latent(x, mod, g_pre, wd, g_q, wq, g_kv, wukv, cos, sin):
    bsz, s, d = x.shape
    tm = TM_PROJ
    hd = MLA_HEADS
    out_shape = (
        jax.ShapeDtypeStruct((bsz, hd, s, QK_PAD), BF16),
        jax.ShapeDtypeStruct((bsz, hd, s, QK_PAD), BF16),
        jax.ShapeDtypeStruct((bsz, hd, s, V_DIM), BF16),
    )
    head_spec = lambda w: pl.BlockSpec((None, hd, tm, w), lambda b, i: (b, 0, i, 0))
    return pl.pallas_call(
        _mla_lat_kernel,
        out_shape=out_shape,
        grid=(bsz, s // tm),
        in_specs=[
            pl.BlockSpec((None, tm, d), lambda b, i: (b, i, 0)),
            _mod_spec(0, 0), _mod_spec(0, 1),
            _resident(g_pre.shape), _resident(wd.shape), _resident(g_q.shape), _resident(wq.shape),
            _resident(g_kv.shape), _resident(wukv.shape),
            pl.BlockSpec((tm, LANES), lambda b, i: (i, 0)),
            pl.BlockSpec((tm, LANES), lambda b, i: (i, 0)),
        ],
        out_specs=(head_spec(QK_PAD), head_spec(QK_PAD), head_spec(V_DIM)),
        compiler_params=_cparams(2),
        name="mla_proj_latent",
    )(x, mod, mod, g_pre, wd, g_q, wq, g_kv, wukv, cos, sin)


def _mla_project_context(ctx, mod, g_pre, wd_kv, g_kv, wukv, row):
    bsz, c, d = ctx.shape
    hd = MLA_HEADS
    out_shape = (
        jax.ShapeDtypeStruct((bsz, hd, c, QK_PAD), BF16),
        jax.ShapeDtypeStruct((bsz, hd, c, V_DIM), BF16),
    )
    head_spec = lambda w: pl.BlockSpec((None, hd, c, w), lambda b: (b, 0, 0, 0))
    return pl.pallas_call(
        functools.partial(_mla_ctx_kernel, row=row),
        out_shape=out_shape,
        grid=(bsz,),
        in_specs=[
            pl.BlockSpec((None, c, d), lambda b: (b, 0, 0)),
            _mod_spec(0, 0), _mod_spec(0, 1),
            _resident(g_pre.shape), _resident(wd_kv.shape), _resident(g_kv.shape),
            _resident(wukv.shape),
        ],
        out_specs=(head_spec(QK_PAD), head_spec(V_DIM)),
        compiler_params=_cparams(1),
        name="mla_proj_context",
    )(ctx, mod, mod, g_pre, wd_kv, g_kv, wukv)


def _attn_kernel(q_ref, kc_ref, kl_ref, vc_ref, vl_ref, o_ref):
    q = q_ref[...]
    tq = q.shape[0]

    def step(k, v, carry):
        m, l, acc = carry
        s = lax.dot_general(q, k, (((1,), (1,)), ((), ())), preferred_element_type=F32)
        m_new = jnp.maximum(m, jnp.max(s, axis=-1, keepdims=True))
        p = jnp.exp(s - m_new)
        alpha = jnp.exp(m - m_new)
        l = alpha * l + jnp.sum(p, axis=-1, keepdims=True)
        acc = alpha * acc + jnp.dot(p.astype(BF16), v, preferred_element_type=F32)
        return m_new, l, acc

    carry = (jnp.full((tq, 1), -jnp.inf, F32), jnp.zeros((tq, 1), F32), jnp.zeros((tq, V_DIM), F32))
    carry = step(kc_ref[...], vc_ref[...], carry)

    def body(j, carry):
        rows = pl.ds(pl.multiple_of(j * TK_ATTN, TK_ATTN), TK_ATTN)
        return step(kl_ref[rows, :], vl_ref[rows, :], carry)

    _, l, acc = lax.fori_loop(0, kl_ref.shape[0] // TK_ATTN, body, carry)
    o_ref[...] = (acc / l).astype(o_ref.dtype)


def _mla_attention(q, k_ctx, k_lat, v_ctx, v_lat):
    bsz, hd, s, _ = q.shape
    c = k_ctx.shape[2]
    tq = TQ_ATTN
    per_head = lambda rows, w: pl.BlockSpec((None, None, rows, w), lambda b, h, i: (b, h, 0, 0))
    return pl.pallas_call(
        _attn_kernel,
        out_shape=jax.ShapeDtypeStruct((bsz, s, hd * V_DIM), BF16),
        grid=(bsz, hd, s // tq),
        in_specs=[
            pl.BlockSpec((None, None, tq, QK_PAD), lambda b, h, i: (b, h, i, 0)),
            per_head(c, QK_PAD), per_head(s, QK_PAD), per_head(c, V_DIM), per_head(s, V_DIM),
        ],
        out_specs=pl.BlockSpec((None, tq, V_DIM), lambda b, h, i: (b, i, h)),
        compiler_params=_cparams(3),
        name="mla_attn",
    )(q, k_ctx, k_lat, v_ctx, v_lat)


def _residual_and_next_norm(x, y, gate, g_post, g_next, shift, scale, x_out_ref, h_out_ref):
    x_new = x + gate * (_rms(y) * g_post)
    x_out_ref[...] = x_new
    if h_out_ref is not None:
        h_out_ref[...] = _modulated_norm(x_new, g_next, shift, scale).astype(BF16)


def _attn_post_kernel(o_ref, x_ref, wo_ref, gate_ref, sh_ref, sc_ref, gpost_ref, gnext_ref,
                      x_out_ref, h_out_ref):
    b = pl.ds(pl.program_id(0), 1)
    y = jnp.dot(o_ref[...], wo_ref[...], preferred_element_type=F32)
    _residual_and_next_norm(x_ref[...], y, gate_ref[b, :], gpost_ref[...], gnext_ref[...],
                            sh_ref[b, :], sc_ref[b, :], x_out_ref, h_out_ref)


def _attn_post(o, x, w_o, mod, layer, g_post, g_next):
    bsz, s, d = x.shape
    tm = TM_POST
    tile = lambda: pl.BlockSpec((None, tm, d), lambda b, i: (b, i, 0))
    return pl.pallas_call(
        _attn_post_kernel,
        out_shape=(jax.ShapeDtypeStruct((bsz, s, d), F32), jax.ShapeDtypeStruct((bsz, s, d), BF16)),
        grid=(bsz, s // tm),
        in_specs=[tile(), tile(), _resident(w_o.shape),
                  _mod_spec(layer, 2), _mod_spec(layer, 3), _mod_spec(layer, 4),
                  _resident(g_post.shape), _resident(g_next.shape)],
        out_specs=(tile(), tile()),
        compiler_params=_cparams(2),
        name="attn_post",
    )(o, x, w_o, mod, mod, mod, g_post, g_next)


def _halo_specs(tm, d, s):
    per = tm // HALO
    last = s // HALO - 1
    main = pl.BlockSpec((None, tm, d), lambda b, i: (b, i, 0))
    prev = pl.BlockSpec((None, HALO, d), lambda b, i: (b, jnp.maximum(i * per - 1, 0), 0))
    nxt = pl.BlockSpec((None, HALO, d), lambda b, i: (b, jnp.minimum((i + 1) * per, last), 0))
    return prev, main, nxt


def _gather_halo(hp_ref, h_ref, hn_ref, hext_ref):
    i = pl.program_id(1)
    tm = h_ref.shape[0]
    hp = hp_ref[...]
    hn = hn_ref[...]
    has_prev = jnp.broadcast_to((i > 0).astype(jnp.int32), hp.shape) > 0
    has_next = jnp.broadcast_to((i < pl.num_programs(1) - 1).astype(jnp.int32), hn.shape) > 0
    hext_ref[0:HALO, :] = jnp.where(has_prev, hp, jnp.zeros_like(hp))
    hext_ref[HALO:HALO + tm, :] = h_ref[...]
    hext_ref[HALO + tm:, :] = jnp.where(has_next, hn, jnp.zeros_like(hn))


def _ffn_kernel(*refs, emit_next):
    (hp_ref, h_ref, hn_ref, x_ref, wup_ref, wdw_ref, bdw_ref, wdn_ref, gate_ref, gpost_ref) = refs[:10]
    if emit_next:
        gnext_ref, sh_ref, sc_ref, x_out_ref, h_out_ref, hext_ref = refs[10:]
    else:
        (x_out_ref, hext_ref), h_out_ref = refs[10:], None
    tm = h_ref.shape[0]
    rows = tm + 2 * HALO
    _gather_halo(hp_ref, h_ref, hn_ref, hext_ref)
    hext = hext_ref[...]

    def conv3(col):
        u = jnp.dot(hext, wup_ref[:, col:col + FFN_CHUNK], preferred_element_type=F32)
        w = wdw_ref[:, col:col + FFN_CHUNK]
        out = (pltpu.roll(u, 1, axis=0)[HALO:HALO + tm] * w[0:1]
               + u[HALO:HALO + tm] * w[1:2]
               + pltpu.roll(u, rows - 1, axis=0)[HALO:HALO + tm] * w[2:3])
        return out + bdw_ref[:, col:col + FFN_CHUNK]

    y = jnp.zeros((tm, D_MODEL), F32)
    for c in range(FFN_DIM // FFN_CHUNK):
        col = c * FFN_CHUNK
        gate = conv3(col)
        val = conv3(FFN_DIM + col)
        z = (gate * jax.nn.sigmoid(gate) * val).astype(BF16)
        y = y + jnp.dot(z, wdn_ref[col:col + FFN_CHUNK, :], preferred_element_type=F32)

    b = pl.ds(pl.program_id(0), 1)
    if emit_next:
        _residual_and_next_norm(x_ref[...], y, gate_ref[b, :], gpost_ref[...], gnext_ref[...],
                                sh_ref[b, :], sc_ref[b, :], x_out_ref, h_out_ref)
    else:
        _residual_and_next_norm(x_ref[...], y, gate_ref[b, :], gpost_ref[...], None, None, None,
                                x_out_ref, None)


def _conv_ffn(h, x, w_up, w_dw, b_dw, w_down, mod, layer, g_post, g_next=None):
    bsz, s, d = x.shape
    tm = TM_FFN
    emit_next = g_next is not None
    tile = lambda: pl.BlockSpec((None, tm, d), lambda b, i: (b, i, 0))
    in_specs = [*_halo_specs(tm, d, s), tile(), _resident(w_up.shape), _resident(w_dw.shape),
                _resident(b_dw.shape), _resident(w_down.shape), _mod_spec(layer, 5),
                _resident(g_post.shape)]
    args = [h, h, h, x, w_up, w_dw, b_dw, w_down, mod, g_post]
    out_shape = [jax.ShapeDtypeStruct((bsz, s, d), F32)]
    out_specs = [tile()]
    if emit_next:
        in_specs += [_resident(g_next.shape), _mod_spec(layer + 1, 0), _mod_spec(layer + 1, 1)]
        args += [g_next, mod, mod]
        out_shape.append(jax.ShapeDtypeStruct((bsz, s, d), BF16))
        out_specs.append(tile())
    return pl.pallas_call(
        functools.partial(_ffn_kernel, emit_next=emit_next),
        out_shape=tuple(out_shape),
        grid=(bsz, s // tm),
        in_specs=in_specs,
        out_specs=tuple(out_specs),
        scratch_shapes=[pltpu.VMEM((tm + 2 * HALO, d), BF16)],
        compiler_params=_cparams(2),
        name="conv_ffn",
    )(*args)


def _conformer_kernel(hp_ref, h_ref, hn_ref, x_ref, wpw1_ref, bpw1_ref, wdw_ref, bdw_ref, gln_ref,
                      bln_ref, wpw2_ref, bpw2_ref, gate_ref, sh_ref, sc_ref, gpost_ref, gnext_ref,
                      x_out_ref, h_out_ref, hext_ref, glu_ref, conv_ref):
    i = pl.program_id(1)
    tm = h_ref.shape[0]
    rows = tm + 2 * HALO
    d = D_MODEL
    n_blk = d // LANES
    _gather_halo(hp_ref, h_ref, hn_ref, hext_ref)

    u = jnp.dot(hext_ref[...], wpw1_ref[...], preferred_element_type=F32) + bpw1_ref[...]
    glu = u[:, :d] * jax.nn.sigmoid(u[:, d:])
    r = lax.broadcasted_iota(jnp.int32, (rows, 1), 0)
    first_row = jnp.where(i > 0, 0, HALO)
    end_row = jnp.where(i < pl.num_programs(1) - 1, rows, HALO + tm)
    inside = jnp.logical_and(r >= first_row, r < end_row)
    glu = jnp.where(inside, glu, 0.0)
    for c in range(n_blk):
        glu_ref[c] = glu[:, c * LANES:(c + 1) * LANES]

    pad = (CONV_WIDTH - 1) // 2

    def conv_block(c, carry):
        w = wdw_ref[c]
        for r0 in range(0, tm, CONF_ROWS):
            acc = jnp.broadcast_to(bdw_ref[c], (CONF_ROWS, LANES))
            for k in range(CONV_WIDTH):
                acc = acc + glu_ref[c, pl.ds(HALO - pad + k + r0, CONF_ROWS), :] * w[k:k + 1, :]
            conv_ref[c, r0:r0 + CONF_ROWS, :] = acc
        return carry

    lax.fori_loop(0, n_blk, conv_block, 0)

    v = jnp.concatenate([conv_ref[c] for c in range(n_blk)], axis=-1)
    mu = jnp.mean(v, axis=-1, keepdims=True)
    var = jnp.mean(jnp.square(v - mu), axis=-1, keepdims=True)
    ln = (v - mu) * lax.rsqrt(var + EPS) * gln_ref[...] + bln_ref[...]
    z = (ln * jax.nn.sigmoid(ln)).astype(BF16)
    y = jnp.dot(z, wpw2_ref[...], preferred_element_type=F32) + bpw2_ref[...]

    b = pl.ds(pl.program_id(0), 1)
    _residual_and_next_norm(x_ref[...], y, gate_ref[b, :], gpost_ref[...], gnext_ref[...],
                            sh_ref[b, :], sc_ref[b, :], x_out_ref, h_out_ref)


def _conformer(h, x, w_pw1, b_pw1, w_dw_blk, b_dw_blk, g_ln, b_ln, w_pw2, b_pw2, mod, layer,
               g_post, g_next):
    bsz, s, d = x.shape
    tm = TM_CONF
    n_blk = d // LANES
    tile = lambda: pl.BlockSpec((None, tm, d), lambda b, i: (b, i, 0))
    consts = [w_pw1, b_pw1, w_dw_blk, b_dw_blk, g_ln, b_ln, w_pw2, b_pw2]
    return pl.pallas_call(
        _conformer_kernel,
        out_shape=(jax.ShapeDtypeStruct((bsz, s, d), F32), jax.ShapeDtypeStruct((bsz, s, d), BF16)),
        grid=(bsz, s // tm),
        in_specs=[*_halo_specs(tm, d, s), tile(), *[_resident(a.shape) for a in consts],
                  _mod_spec(layer, 2), _mod_spec(layer, 3), _mod_spec(layer, 4),
                  _resident(g_post.shape), _resident(g_next.shape)],
        out_specs=(tile(), tile()),
        scratch_shapes=[pltpu.VMEM((tm + 2 * HALO, d), BF16),
                        pltpu.VMEM((n_blk, tm + 2 * HALO, LANES), F32),
                        pltpu.VMEM((n_blk, tm, LANES), F32)],
        compiler_params=_cparams(2),
        name="conformer",
    )(h, h, h, x, *consts, mod, mod, mod, g_post, g_next)


def _rope_tables(seq_len):
    n = ROPE_DIM // 4
    pos = np.arange(seq_len)
    inv = ROPE_THETA ** (-jnp.arange(n, dtype=F32) / n)
    ang_r = jnp.asarray(pos // GRID_W, F32)[:, None] * inv
    ang_c = jnp.asarray(pos % GRID_W, F32)[:, None] * inv
    zeros = jnp.zeros((seq_len, LANES - ROPE_DIM), F32)
    cos = jnp.concatenate([jnp.cos(ang_r), jnp.cos(ang_r), jnp.cos(ang_c), jnp.cos(ang_c), zeros], -1)
    sin = jnp.concatenate([-jnp.sin(ang_r), jnp.sin(ang_r), -jnp.sin(ang_c), jnp.sin(ang_c), zeros], -1)
    return cos, sin


def _swap_halves_perm():
    n = ROPE_DIM // 4
    j = np.arange(ROPE_DIM)
    return (j // (2 * n)) * 2 * n + (1 - (j % (2 * n)) // n) * n + j % n


def _pad_lanes(w):
    return jnp.pad(w, [(0, 0)] * (w.ndim - 1) + [(0, LANES - w.shape[-1])])


def _mla_weights(w_dqkv, w_uq):
    perm = _swap_halves_perm()
    kr = w_dqkv[:, Q_LORA + KV_LORA:]
    wd = jnp.concatenate([w_dqkv[:, :Q_LORA + KV_LORA], _pad_lanes(kr), _pad_lanes(kr[:, perm])], -1)
    wq = w_uq.reshape(Q_LORA, MLA_HEADS, QK_DIM)
    rope = wq[:, :, NOPE_DIM:]
    wq = jnp.concatenate([wq[:, :, :NOPE_DIM].reshape(Q_LORA, -1),
                          _pad_lanes(rope).reshape(Q_LORA, -1),
                          _pad_lanes(rope[:, :, perm]).reshape(Q_LORA, -1)], -1)
    return wd.astype(BF16), wq.astype(BF16)


def kernel(x, c, ctx, c_ctx, w_mod, b_mod, g_pre_mix, g_post_mix, g_pre_ffn, g_post_ffn, mla_w_dqkv, mla_g_q, mla_w_uq, mla_g_kv, mla_w_ukv, mla_w_o, cv_w_pw1, cv_b_pw1, cv_w_dw, cv_b_dw, cv_g_ln, cv_b_ln, cv_w_pw2, cv_b_pw2, ffn_w_up, ffn_w_dw, ffn_b_dw, ffn_w_down):
    bsz, s, d = x.shape
    assert (bsz, d) == (c.shape[0], D_MODEL) and bsz + 1 <= MOD_ROWS and DEPTH == 2
    row = lambda v: v.reshape(1, -1)

    cc = jnp.concatenate([c, c_ctx[None, :], jnp.zeros((MOD_ROWS - bsz - 1, d), F32)], axis=0)
    mod = _adaln_mod(cc, w_mod, b_mod)

    cos, sin = _rope_tables(s)
    wd, wq = _mla_weights(mla_w_dqkv[0], mla_w_uq[0])
    wukv = mla_w_ukv[0].astype(BF16)
    g0 = row(g_pre_mix[0])
    g_kv = row(mla_g_kv[0])
    q, k_lat, v_lat = _mla_project_latent(x, mod, g0, wd, row(mla_g_q[0]), wq, g_kv, wukv, cos, sin)
    k_ctx, v_ctx = _mla_project_context(ctx, mod, g0, wd[:, Q_LORA:Q_LORA + KV_LORA + LANES], g_kv,
                                        wukv, bsz)
    o = _mla_attention(q, k_ctx, k_lat, v_ctx, v_lat)
    x1, h = _attn_post(o, x, mla_w_o[0].astype(BF16), mod, 0, row(g_post_mix[0]), row(g_pre_ffn[0]))
    x2, h = _conv_ffn(h, x1, ffn_w_up[0].astype(BF16), ffn_w_dw[0], row(ffn_b_dw[0]),
                      ffn_w_down[0].astype(BF16), mod, 0, row(g_post_ffn[0]), row(g_pre_mix[1]))

    n_blk = d // LANES
    w_dw_blk = jnp.pad(cv_w_dw[0], ((0, 1), (0, 0))).reshape(CONV_WIDTH + 1, n_blk, LANES)
    w_dw_blk = w_dw_blk.transpose(1, 0, 2)
    b_dw_blk = cv_b_dw[0].reshape(n_blk, 1, LANES)
    x3, h = _conformer(h, x2, cv_w_pw1[0].astype(BF16), row(cv_b_pw1[0]), w_dw_blk, b_dw_blk,
                       row(cv_g_ln[0]), row(cv_b_ln[0]), cv_w_pw2[0].astype(BF16), row(cv_b_pw2[0]),
                       mod, 1, row(g_post_mix[1]), row(g_pre_ffn[1]))
    (x4,) = _conv_ffn(h, x3, ffn_w_up[1].astype(BF16), ffn_w_dw[1], row(ffn_b_dw[1]),
                      ffn_w_down[1].astype(BF16), mod, 1, row(g_post_ffn[1]))
    return x4
```

```python
import functools
import math

import jax
import jax.numpy as jnp
import numpy as np
from jax import lax
from jax.experimental import pallas as pl
from jax.experimental.pallas import tpu as pltpu

F32 = jnp.float32
BF16 = jnp.bfloat16

D_MODEL = 1024
DEPTH = 2
GRID_W = 64
MLA_HEADS = 8
Q_LORA = 384
KV_LORA = 128
NOPE_DIM = 128
ROPE_DIM = 64
V_DIM = 128
QK_DIM = NOPE_DIM + ROPE_DIM
ROPE_THETA = 10000.0
CONV_WIDTH = 31
FFN_DIM = 2816
FFN_CONV_WIDTH = 3
EPS = 1e-6

LANES = 128
QK_PAD = 2 * LANES
V_PAD = 2 * V_DIM
MOD_ROWS = 8
HALO = 16
VMEM_LIMIT = 56 * 1024 * 1024

TM_PROJ = 512
TQ_ATTN = 512
TK_ATTN = 512
TM_POST = 512
TM_FFN = 512
FFN_CHUNK = 256
TM_CONF = 512
CONF_ROWS = 128


def _rms(x):
    return x * lax.rsqrt(jnp.mean(x * x, axis=-1, keepdims=True) + EPS)


def _modulated_norm(x, g, shift, scale):
    return (_rms(x) * g) * (1.0 + scale) + shift


def _cparams(n_axes):
    return pltpu.CompilerParams(
        dimension_semantics=("parallel",) * n_axes, vmem_limit_bytes=VMEM_LIMIT)


def _resident(shape):
    zeros = (0,) * len(shape)
    return pl.BlockSpec(shape, lambda *_: zeros, pipeline_mode=pl.Buffered(1))


def _mod_kernel(cc_ref, w_ref, b_ref, o_ref):
    a = cc_ref[...]
    a = a * jax.nn.sigmoid(a)
    o_ref[...] = jnp.dot(a.astype(BF16), w_ref[...].astype(BF16),
                         preferred_element_type=F32) + b_ref[...]


def _adaln_mod(cc, w_mod, b_mod):
    d = D_MODEL
    return pl.pallas_call(
        _mod_kernel,
        out_shape=jax.ShapeDtypeStruct((DEPTH, 6, MOD_ROWS, d), F32),
        grid=(DEPTH, 6),
        in_specs=[
            pl.BlockSpec((MOD_ROWS, d), lambda i, j: (0, 0)),
            pl.BlockSpec((None, d, d), lambda i, j: (i, 0, j)),
            pl.BlockSpec((None, 1, d), lambda i, j: (i, 0, j)),
        ],
        out_specs=pl.BlockSpec((None, None, MOD_ROWS, d), lambda i, j: (i, j, 0, 0)),
        compiler_params=_cparams(2),
        name="adaln_mod",
    )(cc, w_mod, b_mod.reshape(DEPTH, 1, 6 * d))


def _mod_spec(layer, chunk):
    return pl.BlockSpec((None, None, MOD_ROWS, D_MODEL), lambda *_: (layer, chunk, 0, 0))


def _kv_project(h, wd_kv, g_kv, w_ukv):
    dkv = jnp.dot(h, wd_kv, preferred_element_type=F32)
    nkv = (_rms(dkv[:, :KV_LORA]) * g_kv).astype(BF16)
    kv = jnp.dot(nkv, w_ukv, preferred_element_type=F32)
    return kv, dkv[:, KV_LORA:]


def _store_kv(kv, kr, k_ref, v_ref):
    for hh in range(MLA_HEADS):
        base = hh * (NOPE_DIM + V_DIM)
        k_ref[hh, :, 0:NOPE_DIM] = kv[:, base:base + NOPE_DIM].astype(BF16)
        k_ref[hh, :, NOPE_DIM:QK_PAD] = kr
        v_ref[hh, :, 0:V_DIM] = kv[:, base + NOPE_DIM:base + NOPE_DIM + V_DIM].astype(BF16)
        v_ref[hh, :, V_DIM:] = jnp.ones((kv.shape[0], V_DIM), BF16)


def _mla_lat_kernel(x_ref, sh_ref, sc_ref, g_ref, wd_ref, gq_ref, wq_ref, gkv_ref, wukv_ref,
                    cos_ref, sin_ref, q_ref, k_ref, v_ref):
    b = pl.program_id(0)
    shift = sh_ref[pl.ds(b, 1), :]
    scale = sc_ref[pl.ds(b, 1), :]
    h = _modulated_norm(x_ref[...], g_ref[...], shift, scale).astype(BF16)
    cos = cos_ref[...]
    sin = sin_ref[...]

    kv, kr_raw = _kv_project(h, wd_ref[:, Q_LORA:], gkv_ref[...], wukv_ref[...])
    kr = (kr_raw[:, :LANES] * cos + kr_raw[:, LANES:] * sin).astype(BF16)
    _store_kv(kv, kr, k_ref, v_ref)

    cq = jnp.dot(h, wd_ref[:, :Q_LORA], preferred_element_type=F32)
    nq = (_rms(cq) * (gq_ref[...] * (math.log2(math.e) / math.sqrt(QK_DIM)))).astype(BF16)
    n_nope = MLA_HEADS * NOPE_DIM
    n_rope = MLA_HEADS * LANES
    q_nope = jnp.dot(nq, wq_ref[:, :n_nope], preferred_element_type=F32)
    q_rope = jnp.dot(nq, wq_ref[:, n_nope:n_nope + n_rope], preferred_element_type=F32)
    q_swap = jnp.dot(nq, wq_ref[:, n_nope + n_rope:], preferred_element_type=F32)
    for hh in range(MLA_HEADS):
        q_ref[hh, :, 0:NOPE_DIM] = q_nope[:, hh * NOPE_DIM:(hh + 1) * NOPE_DIM].astype(BF16)
        cols = slice(hh * LANES, (hh + 1) * LANES)
        q_ref[hh, :, NOPE_DIM:QK_PAD] = (q_rope[:, cols] * cos + q_swap[:, cols] * sin).astype(BF16)


def _mla_ctx_kernel(x_ref, sh_ref, sc_ref, g_ref, wd_ref, gkv_ref, wukv_ref, k_ref, v_ref, *, row):
    shift = sh_ref[row:row + 1, :]
    scale = sc_ref[row:row + 1, :]
    h = _modulated_norm(x_ref[...], g_ref[...], shift, scale).astype(BF16)
    kv, kr_raw = _kv_project(h, wd_ref[...], gkv_ref[...], wukv_ref[...])
    _store_kv(kv, kr_raw[:, :LANES].astype(BF16), k_ref, v_ref)


def _mla_project_latent(x, mod, g_pre, wd, g_q, wq, g_kv, wukv, cos, sin):
    bsz, s, d = x.shape
    tm = TM_PROJ
    hd = MLA_HEADS
    out_shape = (
        jax.ShapeDtypeStruct((bsz, hd, s, QK_PAD), BF16),
        jax.ShapeDtypeStruct((bsz, hd, s, QK_PAD), BF16),
        jax.ShapeDtypeStruct((bsz, hd, s, V_PAD), BF16),
    )
    head_spec = lambda w: pl.BlockSpec((None, hd, tm, w), lambda b, i: (b, 0, i, 0))
    return pl.pallas_call(
        _mla_lat_kernel,
        out_shape=out_shape,
        grid=(bsz, s // tm),
        in_specs=[
            pl.BlockSpec((None, tm, d), lambda b, i: (b, i, 0)),
            _mod_spec(0, 0), _mod_spec(0, 1),
            _resident(g_pre.shape), _resident(wd.shape), _resident(g_q.shape), _resident(wq.shape),
            _resident(g_kv.shape), _resident(wukv.shape),
            pl.BlockSpec((tm, LANES), lambda b, i: (i, 0)),
            pl.BlockSpec((tm, LANES), lambda b, i: (i, 0)),
        ],
        out_specs=(head_spec(QK_PAD), head_spec(QK_PAD), head_spec(V_PAD)),
        compiler_params=_cparams(2),
        name="mla_proj_latent",
    )(x, mod, mod, g_pre, wd, g_q, wq, g_kv, wukv, cos, sin)


def _mla_project_context(ctx, mod, g_pre, wd_kv, g_kv, wukv, row):
    bsz, c, d = ctx.shape
    hd = MLA_HEADS
    out_shape = (
        jax.ShapeDtypeStruct((bsz, hd, c, QK_PAD), BF16),
        jax.ShapeDtypeStruct((bsz, hd, c, V_PAD), BF16),
    )
    head_spec = lambda w: pl.BlockSpec((None, hd, c, w), lambda b: (b, 0, 0, 0))
    return pl.pallas_call(
        functools.partial(_mla_ctx_kernel, row=row),
        out_shape=out_shape,
        grid=(bsz,),
        in_specs=[
            pl.BlockSpec((None, c, d), lambda b: (b, 0, 0)),
            _mod_spec(0, 0), _mod_spec(0, 1),
            _resident(g_pre.shape), _resident(wd_kv.shape), _resident(g_kv.shape),
            _resident(wukv.shape),
        ],
        out_specs=(head_spec(QK_PAD), head_spec(V_PAD)),
        compiler_params=_cparams(1),
        name="mla_proj_context",
    )(ctx, mod, mod, g_pre, wd_kv, g_kv, wukv)


def _attn_kernel(q_ref, kc_ref, kl_ref, vc_ref, vl_ref, o_ref):
    q = q_ref[...]
    tq = q.shape[0]

    def step(k, v, carry):
        m, acc = carry
        s = lax.dot_general(q, k, (((1,), (1,)), ((), ())), preferred_element_type=F32)
        m_new = jnp.maximum(m, jnp.max(s, axis=-1, keepdims=True))
        p = jnp.exp2(s - m_new)
        alpha = jnp.exp2(m - m_new)
        acc = alpha * acc + jnp.dot(p.astype(BF16), v, preferred_element_type=F32)
        return m_new, acc

    carry = (jnp.full((tq, 1), -jnp.inf, F32), jnp.zeros((tq, 2 * V_DIM), F32))
    carry = step(kc_ref[...], vc_ref[...], carry)
    for j in range(kl_ref.shape[0] // TK_ATTN):
        rows = slice(j * TK_ATTN, (j + 1) * TK_ATTN)
        carry = step(kl_ref[rows, :], vl_ref[rows, :], carry)
    _, acc = carry
    o_ref[...] = (acc[:, :V_DIM] / acc[:, V_DIM:]).astype(o_ref.dtype)


def _mla_attention(q, k_ctx, k_lat, v_ctx, v_lat):
    bsz, hd, s, _ = q.shape
    c = k_ctx.shape[2]
    tq = TQ_ATTN
    per_head = lambda rows, w: pl.BlockSpec((None, None, rows, w), lambda b, h, i: (b, h, 0, 0))
    return pl.pallas_call(
        _attn_kernel,
        out_shape=jax.ShapeDtypeStruct((bsz, s, hd * V_DIM), BF16),
        grid=(bsz, hd, s // tq),
        in_specs=[
            pl.BlockSpec((None, None, tq, QK_PAD), lambda b, h, i: (b, h, i, 0)),
            per_head(c, QK_PAD), per_head(s, QK_PAD), per_head(c, V_PAD), per_head(s, V_PAD),
        ],
        out_specs=pl.BlockSpec((None, tq, V_DIM), lambda b, h, i: (b, i, h)),
        compiler_params=_cparams(3),
        name="mla_attn",
    )(q, k_ctx, k_lat, v_ctx, v_lat)


def _residual_and_next_norm(x, y, gate, g_post, g_next, shift, scale, x_out_ref, h_out_ref):
    x_new = x + gate * (_rms(y) * g_post)
    x_out_ref[...] = x_new
    if h_out_ref is not None:
        h_out_ref[...] = _modulated_norm(x_new, g_next, shift, scale).astype(BF16)


def _attn_post_kernel(o_ref, x_ref, wo_ref, gate_ref, sh_ref, sc_ref, gpost_ref, gnext_ref,
                      x_out_ref, h_out_ref):
    b = pl.ds(pl.program_id(0), 1)
    y = jnp.dot(o_ref[...], wo_ref[...], preferred_element_type=F32)
    _residual_and_next_norm(x_ref[...], y, gate_ref[b, :], gpost_ref[...], gnext_ref[...],
                            sh_ref[b, :], sc_ref[b, :], x_out_ref, h_out_ref)


def _attn_post(o, x, w_o, mod, layer, g_post, g_next):
    bsz, s, d = x.shape
    tm = TM_POST
    tile = lambda: pl.BlockSpec((None, tm, d), lambda b, i: (b, i, 0))
    return pl.pallas_call(
        _attn_post_kernel,
        out_shape=(jax.ShapeDtypeStruct((bsz, s, d), F32), jax.ShapeDtypeStruct((bsz, s, d), BF16)),
        grid=(bsz, s // tm),
        in_specs=[tile(), tile(), _resident(w_o.shape),
                  _mod_spec(layer, 2), _mod_spec(layer, 3), _mod_spec(layer, 4),
                  _resident(g_post.shape), _resident(g_next.shape)],
        out_specs=(tile(), tile()),
        compiler_params=_cparams(2),
        name="attn_post",
    )(o, x, w_o, mod, mod, mod, g_post, g_next)


def _halo_specs(tm, d, s):
    per = tm // HALO
    last = s // HALO - 1
    main = pl.BlockSpec((None, tm, d), lambda b, i: (b, i, 0))
    prev = pl.BlockSpec((None, HALO, d), lambda b, i: (b, jnp.maximum(i * per - 1, 0), 0))
    nxt = pl.BlockSpec((None, HALO, d), lambda b, i: (b, jnp.minimum((i + 1) * per, last), 0))
    return prev, main, nxt


def _gather_halo(hp_ref, h_ref, hn_ref, hext_ref):
    i = pl.program_id(1)
    tm = h_ref.shape[0]
    hp = hp_ref[...]
    hn = hn_ref[...]
    has_prev = jnp.broadcast_to((i > 0).astype(jnp.int32), hp.shape) > 0
    has_next = jnp.broadcast_to((i < pl.num_programs(1) - 1).astype(jnp.int32), hn.shape) > 0
    hext_ref[0:HALO, :] = jnp.where(has_prev, hp, jnp.zeros_like(hp))
    hext_ref[HALO:HALO + tm, :] = h_ref[...]
    hext_ref[HALO + tm:, :] = jnp.where(has_next, hn, jnp.zeros_like(hn))


def _ffn_kernel(*refs, emit_next):
    (hp_ref, h_ref, hn_ref, x_ref, wup_ref, wdw_ref, bdw_ref, wdn_ref, gate_ref, gpost_ref) = refs[:10]
    if emit_next:
        gnext_ref, sh_ref, sc_ref, x_out_ref, h_out_ref, hext_ref = refs[10:]
    else:
        (x_out_ref, hext_ref), h_out_ref = refs[10:], None
    tm = h_ref.shape[0]
    rows = tm + 2 * HALO
    _gather_halo(hp_ref, h_ref, hn_ref, hext_ref)
    hext = hext_ref[...]

    def conv3(col):
        u = jnp.dot(hext, wup_ref[:, col:col + FFN_CHUNK], preferred_element_type=F32)
        w = wdw_ref[:, col:col + FFN_CHUNK]
        out = (pltpu.roll(u, 1, axis=0)[HALO:HALO + tm] * w[0:1]
               + u[HALO:HALO + tm] * w[1:2]
               + pltpu.roll(u, rows - 1, axis=0)[HALO:HALO + tm] * w[2:3])
        return out + bdw_ref[:, col:col + FFN_CHUNK]

    y = jnp.zeros((tm, D_MODEL), F32)
    for c in range(FFN_DIM // FFN_CHUNK):
        col = c * FFN_CHUNK
        gate = conv3(col)
        val = conv3(FFN_DIM + col)
        z = (gate * jax.nn.sigmoid(gate) * val).astype(BF16)
        y = y + jnp.dot(z, wdn_ref[col:col + FFN_CHUNK, :], preferred_element_type=F32)

    b = pl.ds(pl.program_id(0), 1)
    if emit_next:
        _residual_and_next_norm(x_ref[...], y, gate_ref[b, :], gpost_ref[...], gnext_ref[...],
                                sh_ref[b, :], sc_ref[b, :], x_out_ref, h_out_ref)
    else:
        _residual_and_next_norm(x_ref[...], y, gate_ref[b, :], gpost_ref[...], None, None, None,
                                x_out_ref, None)


def _conv_ffn(h, x, w_up, w_dw, b_dw, w_down, mod, layer, g_post, g_next=None):
    bsz, s, d = x.shape
    tm = TM_FFN
    emit_next = g_next is not None
    tile = lambda: pl.BlockSpec((None, tm, d), lambda b, i: (b, i, 0))
    in_specs = [*_halo_specs(tm, d, s), tile(), _resident(w_up.shape), _resident(w_dw.shape),
                _resident(b_dw.shape), _resident(w_down.shape), _mod_spec(layer, 5),
                _resident(g_post.shape)]
    args = [h, h, h, x, w_up, w_dw, b_dw, w_down, mod, g_post]
    out_shape = [jax.ShapeDtypeStruct((bsz, s, d), F32)]
    out_specs = [tile()]
    if emit_next:
        in_specs += [_resident(g_next.shape), _mod_spec(layer + 1, 0), _mod_spec(layer + 1, 1)]
        args += [g_next, mod, mod]
        out_shape.append(jax.ShapeDtypeStruct((bsz, s, d), BF16))
        out_specs.append(tile())
    return pl.pallas_call(
        functools.partial(_ffn_kernel, emit_next=emit_next),
        out_shape=tuple(out_shape),
        grid=(bsz, s // tm),
        in_specs=in_specs,
        out_specs=tuple(out_specs),
        scratch_shapes=[pltpu.VMEM((tm + 2 * HALO, d), BF16)],
        compiler_params=_cparams(2),
        name="conv_ffn",
    )(*args)


def _conformer_kernel(hp_ref, h_ref, hn_ref, x_ref, wpw1_ref, bpw1_ref, wdw_ref, bdw_ref, gln_ref,
                      bln_ref, wpw2_ref, bpw2_ref, gate_ref, sh_ref, sc_ref, gpost_ref, gnext_ref,
                      x_out_ref, h_out_ref, hext_ref, glu_ref, conv_ref):
    i = pl.program_id(1)
    tm = h_ref.shape[0]
    rows = tm + 2 * HALO
    d = D_MODEL
    n_blk = d // LANES
    _gather_halo(hp_ref, h_ref, hn_ref, hext_ref)

    u = jnp.dot(hext_ref[...], wpw1_ref[...], preferred_element_type=F32) + bpw1_ref[...]
    glu = u[:, :d] * jax.nn.sigmoid(u[:, d:])
    r = lax.broadcasted_iota(jnp.int32, (rows, 1), 0)
    first_row = jnp.where(i > 0, 0, HALO)
    end_row = jnp.where(i < pl.num_programs(1) - 1, rows, HALO + tm)
    inside = jnp.logical_and(r >= first_row, r < end_row)
    glu = jnp.where(inside, glu, 0.0)
    for c in range(n_blk):
        glu_ref[c] = glu[:, c * LANES:(c + 1) * LANES]

    pad = (CONV_WIDTH - 1) // 2

    def conv_block(c, carry):
        w = wdw_ref[c]
        for r0 in range(0, tm, CONF_ROWS):
            acc = jnp.broadcast_to(bdw_ref[c], (CONF_ROWS, LANES))
            for k in range(CONV_WIDTH):
                acc = acc + glu_ref[c, pl.ds(HALO - pad + k + r0, CONF_ROWS), :] * w[k:k + 1, :]
            conv_ref[c, r0:r0 + CONF_ROWS, :] = acc
        return carry

    lax.fori_loop(0, n_blk, conv_block, 0)

    v = jnp.concatenate([conv_ref[c] for c in range(n_blk)], axis=-1)
    mu = jnp.mean(v, axis=-1, keepdims=True)
    var = jnp.mean(jnp.square(v - mu), axis=-1, keepdims=True)
    ln = (v - mu) * lax.rsqrt(var + EPS) * gln_ref[...] + bln_ref[...]
    z = (ln * jax.nn.sigmoid(ln)).astype(BF16)
    y = jnp.dot(z, wpw2_ref[...], preferred_element_type=F32) + bpw2_ref[...]

    b = pl.ds(pl.program_id(0), 1)
    _residual_and_next_norm(x_ref[...], y, gate_ref[b, :], gpost_ref[...], gnext_ref[...],
                            sh_ref[b, :], sc_ref[b, :], x_out_ref, h_out_ref)


def _conformer(h, x, w_pw1, b_pw1, w_dw_blk, b_dw_blk, g_ln, b_ln, w_pw2, b_pw2, mod, layer,
               g_post, g_next):
    bsz, s, d = x.shape
    tm = TM_CONF
    n_blk = d // LANES
    tile = lambda: pl.BlockSpec((None, tm, d), lambda b, i: (b, i, 0))
    consts = [w_pw1, b_pw1, w_dw_blk, b_dw_blk, g_ln, b_ln, w_pw2, b_pw2]
    return pl.pallas_call(
        _conformer_kernel,
        out_shape=(jax.ShapeDtypeStruct((bsz, s, d), F32), jax.ShapeDtypeStruct((bsz, s, d), BF16)),
        grid=(bsz, s // tm),
        in_specs=[*_halo_specs(tm, d, s), tile(), *[_resident(a.shape) for a in consts],
                  _mod_spec(layer, 2), _mod_spec(layer, 3), _mod_spec(layer, 4),
                  _resident(g_post.shape), _resident(g_next.shape)],
        out_specs=(tile(), tile()),
        scratch_shapes=[pltpu.VMEM((tm + 2 * HALO, d), BF16),
                        pltpu.VMEM((n_blk, tm + 2 * HALO, LANES), F32),
                        pltpu.VMEM((n_blk, tm, LANES), F32)],
        compiler_params=_cparams(2),
        name="conformer",
    )(h, h, h, x, *consts, mod, mod, mod, g_post, g_next)


def _rope_tables(seq_len):
    n = ROPE_DIM // 4
    pos = np.arange(seq_len)
    inv = ROPE_THETA ** (-jnp.arange(n, dtype=F32) / n)
    ang_r = jnp.asarray(pos // GRID_W, F32)[:, None] * inv
    ang_c = jnp.asarray(pos % GRID_W, F32)[:, None] * inv
    zeros = jnp.zeros((seq_len, LANES - ROPE_DIM), F32)
    cos = jnp.concatenate([jnp.cos(ang_r), jnp.cos(ang_r), jnp.cos(ang_c), jnp.cos(ang_c), zeros], -1)
    sin = jnp.concatenate([-jnp.sin(ang_r), jnp.sin(ang_r), -jnp.sin(ang_c), jnp.sin(ang_c), zeros], -1)
    return cos, sin


def _swap_halves_perm():
    n = ROPE_DIM // 4
    j = np.arange(ROPE_DIM)
    return (j // (2 * n)) * 2 * n + (1 - (j % (2 * n)) // n) * n + j % n


def _pad_lanes(w):
    return jnp.pad(w, [(0, 0)] * (w.ndim - 1) + [(0, LANES - w.shape[-1])])


def _mla_weights(w_dqkv, w_uq):
    perm = _swap_halves_perm()
    kr = w_dqkv[:, Q_LORA + KV_LORA:]
    wd = jnp.concatenate([w_dqkv[:, :Q_LORA + KV_LORA], _pad_lanes(kr), _pad_lanes(kr[:, perm])], -1)
    wq = w_uq.reshape(Q_LORA, MLA_HEADS, QK_DIM)
    rope = wq[:, :, NOPE_DIM:]
    wq = jnp.concatenate([wq[:, :, :NOPE_DIM].reshape(Q_LORA, -1),
                          _pad_lanes(rope).reshape(Q_LORA, -1),
                          _pad_lanes(rope[:, :, perm]).reshape(Q_LORA, -1)], -1)
    return wd.astype(BF16), wq.astype(BF16)


def kernel(x, c, ctx, c_ctx, w_mod, b_mod, g_pre_mix, g_post_mix, g_pre_ffn, g_post_ffn, mla_w_dqkv, mla_g_q, mla_w_uq, mla_g_kv, mla_w_ukv, mla_w_o, cv_w_pw1, cv_b_pw1, cv_w_dw, cv_b_dw, cv_g_ln, cv_b_ln, cv_w_pw2, cv_b_pw2, ffn_w_up, ffn_w_dw, ffn_b_dw, ffn_w_down):
    bsz, s, d = x.shape
    assert (bsz, d) == (c.shape[0], D_MODEL) and bsz + 1 <= MOD_ROWS and DEPTH == 2
    row = lambda v: v.reshape(1, -1)

    cc = jnp.concatenate([c, c_ctx[None, :], jnp.zeros((MOD_ROWS - bsz - 1, d), F32)], axis=0)
    mod = _adaln_mod(cc, w_mod, b_mod)

    cos, sin = _rope_tables(s)
    wd, wq = _mla_weights(mla_w_dqkv[0], mla_w_uq[0])
    wukv = mla_w_ukv[0].astype(BF16)
    g0 = row(g_pre_mix[0])
    g_kv = row(mla_g_kv[0])
    q, k_lat, v_lat = _mla_project_latent(x, mod, g0, wd, row(mla_g_q[0]), wq, g_kv, wukv, cos, sin)
    k_ctx, v_ctx = _mla_project_context(ctx, mod, g0, wd[:, Q_LORA:Q_LORA + KV_LORA + LANES], g_kv,
                                        wukv, bsz)
    o = _mla_attention(q, k_ctx, k_lat, v_ctx, v_lat)
    x1, h = _attn_post(o, x, mla_w_o[0].astype(BF16), mod, 0, row(g_post_mix[0]), row(g_pre_ffn[0]))
    x2, h = _conv_ffn(h, x1, ffn_w_up[0].astype(BF16), ffn_w_dw[0], row(ffn_b_dw[0]),
                      ffn_w_down[0].astype(BF16), mod, 0, row(g_post_ffn[0]), row(g_pre_mix[1]))

    n_blk = d // LANES
    w_dw_blk = jnp.pad(cv_w_dw[0], ((0, 1), (0, 0))).reshape(CONV_WIDTH + 1, n_blk, LANES)
    w_dw_blk = w_dw_blk.transpose(1, 0, 2)
    b_dw_blk = cv_b_dw[0].reshape(n_blk, 1, LANES)
    x3, h = _conformer(h, x2, cv_w_pw1[0].astype(BF16), row(cv_b_pw1[0]), w_dw_blk, b_dw_blk,
                       row(cv_g_ln[0]), row(cv_b_ln[0]), cv_w_pw2[0].astype(BF16), row(cv_b_pw2[0]),
                       mod, 1, row(g_post_mix[1]), row(g_pre_ffn[1]))
    (x4,) = _conv_ffn(h, x3, ffn_w_up[1].astype(BF16), ffn_w_dw[1], row(ffn_b_dw[1]),
                      ffn_w_down[1].astype(BF16), mod, 1, row(g_post_ffn[1]))
    return x4
```

```python
import functools
import math

import jax
import jax.numpy as jnp
import numpy as np
from jax import lax
from jax.experimental import pallas as pl
from jax.experimental.pallas import tpu as pltpu

F32 = jnp.float32
BF16 = jnp.bfloat16

D_MODEL = 1024
DEPTH = 2
GRID_W = 64
MLA_HEADS = 8
Q_LORA = 384
KV_LORA = 128
NOPE_DIM = 128
ROPE_DIM = 64
V_DIM = 128
QK_DIM = NOPE_DIM + ROPE_DIM
ROPE_THETA = 10000.0
CONV_WIDTH = 31
FFN_DIM = 2816
FFN_CONV_WIDTH = 3
EPS = 1e-6

LANES = 128
QK_PAD = 2 * LANES
V_PAD = 2 * V_DIM
MOD_ROWS = 8
HALO = 16
VMEM_LIMIT = 56 * 1024 * 1024

TM_PROJ = 512
TQ_ATTN = 512
TK_ATTN = 512
TM_POST = 512
TM_FFN = 512
FFN_CHUNK = 256
TM_CONF = 512
CONF_ROWS = 128


def _rms(x):
    return x * lax.rsqrt(jnp.mean(x * x, axis=-1, keepdims=True) + EPS)


def _modulated_norm(x, g, shift, scale):
    return (_rms(x) * g) * (1.0 + scale) + shift


def _cparams(n_axes):
    return pltpu.CompilerParams(
        dimension_semantics=("parallel",) * n_axes, vmem_limit_bytes=VMEM_LIMIT)


def _resident(shape):
    zeros = (0,) * len(shape)
    return pl.BlockSpec(shape, lambda *_: zeros, pipeline_mode=pl.Buffered(1))


def _mod_kernel(cc_ref, w_ref, b_ref, o_ref):
    a = cc_ref[...]
    a = a * jax.nn.sigmoid(a)
    o_ref[...] = jnp.dot(a.astype(BF16), w_ref[...].astype(BF16),
                         preferred_element_type=F32) + b_ref[...]


def _adaln_mod(cc, w_mod, b_mod):
    d = D_MODEL
    return pl.pallas_call(
        _mod_kernel,
        out_shape=jax.ShapeDtypeStruct((DEPTH, 6, MOD_ROWS, d), F32),
        grid=(DEPTH, 6),
        in_specs=[
            pl.BlockSpec((MOD_ROWS, d), lambda i, j: (0, 0)),
            pl.BlockSpec((None, d, d), lambda i, j: (i, 0, j)),
            pl.BlockSpec((None, 1, d), lambda i, j: (i, 0, j)),
        ],
        out_specs=pl.BlockSpec((None, None, MOD_ROWS, d), lambda i, j: (i, j, 0, 0)),
        compiler_params=_cparams(2),
        name="adaln_mod",
    )(cc, w_mod, b_mod.reshape(DEPTH, 1, 6 * d))


def _mod_spec(layer, chunk):
    return pl.BlockSpec((None, None, MOD_ROWS, D_MODEL), lambda *_: (layer, chunk, 0, 0))


def _kv_project(h, wd_kv, g_kv, w_ukv):
    dkv = jnp.dot(h, wd_kv, preferred_element_type=F32)
    nkv = (_rms(dkv[:, :KV_LORA]) * g_kv).astype(BF16)
    kv = jnp.dot(nkv, w_ukv, preferred_element_type=F32)
    return kv, dkv[:, KV_LORA:]


def _store_kv(kv, kr, k_ref, v_ref):
    for hh in range(MLA_HEADS):
        base = hh * (NOPE_DIM + V_DIM)
        k_ref[hh, :, 0:NOPE_DIM] = kv[:, base:base + NOPE_DIM].astype(BF16)
        k_ref[hh, :, NOPE_DIM:QK_PAD] = kr
        v_ref[hh, :, 0:V_DIM] = kv[:, base + NOPE_DIM:base + NOPE_DIM + V_DIM].astype(BF16)
        v_ref[hh, :, V_DIM:] = jnp.ones((kv.shape[0], V_DIM), BF16)


def _mla_lat_kernel(x_ref, sh_ref, sc_ref, g_ref, wd_ref, gq_ref, wq_ref, gkv_ref, wukv_ref,
                    cos_ref, sin_ref, q_ref, k_ref, v_ref):
    b = pl.program_id(0)
    shift = sh_ref[pl.ds(b, 1), :]
    scale = sc_ref[pl.ds(b, 1), :]
    h = _modulated_norm(x_ref[...], g_ref[...], shift, scale).astype(BF16)
    cos = cos_ref[...]
    sin = sin_ref[...]

    kv, kr_raw = _kv_project(h, wd_ref[:, Q_LORA:], gkv_ref[...], wukv_ref[...])
    kr = (kr_raw[:, :LANES] * cos + kr_raw[:, LANES:] * sin).astype(BF16)
    _store_kv(kv, kr, k_ref, v_ref)

    cq = jnp.dot(h, wd_ref[:, :Q_LORA], preferred_element_type=F32)
    nq = (_rms(cq) * (gq_ref[...] * (math.log2(math.e) / math.sqrt(QK_DIM)))).astype(BF16)
    n_nope = MLA_HEADS * NOPE_DIM
    n_rope = MLA_HEADS * LANES
    q_nope = jnp.dot(nq, wq_ref[:, :n_nope], preferred_element_type=F32)
    q_rope = jnp.dot(nq, wq_ref[:, n_nope:n_nope + n_rope], preferred_element_type=F32)
    q_swap = jnp.dot(nq, wq_ref[:, n_nope + n_rope:], preferred_element_type=F32)
    for hh in range(MLA_HEADS):
        q_ref[hh, :, 0:NOPE_DIM] = q_nope[:, hh * NOPE_DIM:(hh + 1) * NOPE_DIM].astype(BF16)
        cols = slice(hh * LANES, (hh + 1) * LANES)
        q_ref[hh, :, NOPE_DIM:QK_PAD] = (q_rope[:, cols] * cos + q_swap[:, cols] * sin).astype(BF16)


def _mla_ctx_kernel(x_ref, sh_ref, sc_ref, g_ref, wd_ref, gkv_ref, wukv_ref, k_ref, v_ref, *, row):
    shift = sh_ref[row:row + 1, :]
    scale = sc_ref[row:row + 1, :]
    h = _modulated_norm(x_ref[...], g_ref[...], shift, scale).astype(BF16)
    kv, kr_raw = _kv_project(h, wd_ref[...], gkv_ref[...], wukv_ref[...])
    _store_kv(kv, kr_raw[:, :LANES].astype(BF16), k_ref, v_ref)


def _mla_project_latent(x, mod, g_pre, wd, g_q, wq, g_kv, wukv, cos, sin):
    bsz, s, d = x.shape
    tm = TM_PROJ
    hd = MLA_HEADS
    out_shape = (
        jax.ShapeDtypeStruct((bsz, hd, s, QK_PAD), BF16),
        jax.ShapeDtypeStruct((bsz, hd, s, QK_PAD), BF16),
        jax.ShapeDtypeStruct((bsz, hd, s, V_PAD), BF16),
    )
    head_spec = lambda w: pl.BlockSpec((None, hd, tm, w), lambda b, i: (b, 0, i, 0))
    return pl.pallas_call(
        _mla_lat_kernel,
        out_shape=out_shape,
        grid=(bsz, s // tm),
        in_specs=[
            pl.BlockSpec((None, tm, d), lambda b, i: (b, i, 0)),
            _mod_spec(0, 0), _mod_spec(0, 1),
            _resident(g_pre.shape), _resident(wd.shape), _resident(g_q.shape), _resident(wq.shape),
            _resident(g_kv.shape), _resident(wukv.shape),
            pl.BlockSpec((tm, LANES), lambda b, i: (i, 0)),
            pl.BlockSpec((tm, LANES), lambda b, i: (i, 0)),
        ],
        out_specs=(head_spec(QK_PAD), head_spec(QK_PAD), head_spec(V_PAD)),
        compiler_params=_cparams(2),
        name="mla_proj_latent",
    )(x, mod, mod, g_pre, wd, g_q, wq, g_kv, wukv, cos, sin)


def _mla_project_context(ctx, mod, g_pre, wd_kv, g_kv, wukv, row):
    bsz, c, d = ctx.shape
    hd = MLA_HEADS
    out_shape = (
        jax.ShapeDtypeStruct((bsz, hd, c, QK_PAD), BF16),
        jax.ShapeDtypeStruct((bsz, hd, c, V_PAD), BF16),
    )
    head_spec = lambda w: pl.BlockSpec((None, hd, c, w), lambda b: (b, 0, 0, 0))
    return pl.pallas_call(
        functools.partial(_mla_ctx_kernel, row=row),
        out_shape=out_shape,
        grid=(bsz,),
        in_specs=[
            pl.BlockSpec((None, c, d), lambda b: (b, 0, 0)),
            _mod_spec(0, 0), _mod_spec(0, 1),
            _resident(g_pre.shape), _resident(wd_kv.shape), _resident(g_kv.shape),
            _resident(wukv.shape),
        ],
        out_specs=(head_spec(QK_PAD), head_spec(V_PAD)),
        compiler_params=_cparams(1),
        name="mla_proj_context",
    )(ctx, mod, mod, g_pre, wd_kv, g_kv, wukv)


def _attn_kernel(q_ref, kc_ref, kl_ref, vc_ref, vl_ref, o_ref):
    q = q_ref[...]
    tq = q.shape[0]

    def step(k, v, carry):
        m, acc = carry
        s = lax.dot_general(q, k, (((1,), (1,)), ((), ())), preferred_element_type=F32)
        m_new = jnp.maximum(m, jnp.max(s, axis=-1, keepdims=True))
        p = jnp.exp2(s - m_new)
        alpha = jnp.exp2(m - m_new)
        acc = alpha * acc + jnp.dot(p.astype(BF16), v, preferred_element_type=F32)
        return m_new, acc

    carry = (jnp.full((tq, 1), -jnp.inf, F32), jnp.zeros((tq, 2 * V_DIM), F32))
    carry = step(kc_ref[...], vc_ref[...], carry)
    for j in range(kl_ref.shape[0] // TK_ATTN):
        rows = slice(j * TK_ATTN, (j + 1) * TK_ATTN)
        carry = step(kl_ref[rows, :], vl_ref[rows, :], carry)
    _, acc = carry
    o_ref[...] = (acc[:, :V_DIM] / acc[:, V_DIM:]).astype(o_ref.dtype)


def _mla_attention(q, k_ctx, k_lat, v_ctx, v_lat):
    bsz, hd, s, _ = q.shape
    c = k_ctx.shape[2]
    tq = TQ_ATTN
    per_head = lambda rows, w: pl.BlockSpec((None, None, rows, w), lambda b, h, i: (b, h, 0, 0))
    return pl.pallas_call(
        _attn_kernel,
        out_shape=jax.ShapeDtypeStruct((bsz, s, hd * V_DIM), BF16),
        grid=(bsz, hd, s // tq),
        in_specs=[
            pl.BlockSpec((None, None, tq, QK_PAD), lambda b, h, i: (b, h, i, 0)),
            per_head(c, QK_PAD), per_head(s, QK_PAD), per_head(c, V_PAD), per_head(s, V_PAD),
        ],
        out_specs=pl.BlockSpec((None, tq, V_DIM), lambda b, h, i: (b, i, h)),
        compiler_params=_cparams(3),
        name="mla_attn",
    )(q, k_ctx, k_lat, v_ctx, v_lat)


def _residual_and_next_norm(x, y, gate, g_post, g_next, shift, scale, x_out_ref, h_out_ref):
    x_new = x + gate * (_rms(y) * g_post)
    x_out_ref[...] = x_new
    if h_out_ref is not None:
        h_out_ref[...] = _modulated_norm(x_new, g_next, shift, scale).astype(BF16)


def _attn_post_kernel(o_ref, x_ref, wo_ref, gate_ref, sh_ref, sc_ref, gpost_ref, gnext_ref,
                      x_out_ref, h_out_ref):
    b = pl.ds(pl.program_id(0), 1)
    y = jnp.dot(o_ref[...], wo_ref[...], preferred_element_type=F32)
    _residual_and_next_norm(x_ref[...], y, gate_ref[b, :], gpost_ref[...], gnext_ref[...],
                            sh_ref[b, :], sc_ref[b, :], x_out_ref, h_out_ref)


def _attn_post(o, x, w_o, mod, layer, g_post, g_next):
    bsz, s, d = x.shape
    tm = TM_POST
    tile = lambda: pl.BlockSpec((None, tm, d), lambda b, i: (b, i, 0))
    return pl.pallas_call(
        _attn_post_kernel,
        out_shape=(jax.ShapeDtypeStruct((bsz, s, d), F32), jax.ShapeDtypeStruct((bsz, s, d), BF16)),
        grid=(bsz, s // tm),
        in_specs=[tile(), tile(), _resident(w_o.shape),
                  _mod_spec(layer, 2), _mod_spec(layer, 3), _mod_spec(layer, 4),
                  _resident(g_post.shape), _resident(g_next.shape)],
        out_specs=(tile(), tile()),
        compiler_params=_cparams(2),
        name="attn_post",
    )(o, x, w_o, mod, mod, mod, g_post, g_next)


def _halo_specs(tm, d, s):
    per = tm // HALO
    last = s // HALO - 1
    main = pl.BlockSpec((None, tm, d), lambda b, i: (b, i, 0))
    prev = pl.BlockSpec((None, HALO, d), lambda b, i: (b, jnp.maximum(i * per - 1, 0), 0))
    nxt = pl.BlockSpec((None, HALO, d), lambda b, i: (b, jnp.minimum((i + 1) * per, last), 0))
    return prev, main, nxt


def _gather_halo(hp_ref, h_ref, hn_ref, hext_ref):
    i = pl.program_id(1)
    tm = h_ref.shape[0]
    hp = hp_ref[...]
    hn = hn_ref[...]
    has_prev = jnp.broadcast_to((i > 0).astype(jnp.int32), hp.shape) > 0
    has_next = jnp.broadcast_to((i < pl.num_programs(1) - 1).astype(jnp.int32), hn.shape) > 0
    hext_ref[0:HALO, :] = jnp.where(has_prev, hp, jnp.zeros_like(hp))
    hext_ref[HALO:HALO + tm, :] = h_ref[...]
    hext_ref[HALO + tm:, :] = jnp.where(has_next, hn, jnp.zeros_like(hn))


def _ffn_kernel(*refs, emit_next):
    (hp_ref, h_ref, hn_ref, x_ref, wup_ref, wdw_ref, bdw_ref, wdn_ref, gate_ref, gpost_ref) = refs[:10]
    if emit_next:
        gnext_ref, sh_ref, sc_ref, x_out_ref, h_out_ref, hext_ref, ua_ref, ub_ref, acc_ref = refs[10:]
    else:
        (x_out_ref, hext_ref, ua_ref, ub_ref, acc_ref), h_out_ref = refs[10:], None
    tm = h_ref.shape[0]
    n_chunks = FFN_DIM // FFN_CHUNK
    assert n_chunks % 2 == 1
    _gather_halo(hp_ref, h_ref, hn_ref, hext_ref)

    def up_project(c, u_ref):
        hext = hext_ref[...]
        u_ref[0] = jnp.dot(hext, wup_ref[c], preferred_element_type=F32)
        u_ref[1] = jnp.dot(hext, wup_ref[n_chunks + c], preferred_element_type=F32)

    def conv3(c, u_ref, j):
        w = wdw_ref[j * n_chunks + c]
        return (u_ref[j, pl.ds(HALO - 1, tm), :] * w[0:1]
                + u_ref[j, pl.ds(HALO, tm), :] * w[1:2]
                + u_ref[j, pl.ds(HALO + 1, tm), :] * w[2:3]) + bdw_ref[j * n_chunks + c]

    def down_project(c, u_ref):
        gate = conv3(c, u_ref, 0)
        val = conv3(c, u_ref, 1)
        z = (gate * jax.nn.sigmoid(gate) * val).astype(BF16)
        rows = pl.ds(pl.multiple_of(c * FFN_CHUNK, FFN_CHUNK), FFN_CHUNK)
        acc_ref[...] += jnp.dot(z, wdn_ref[rows, :], preferred_element_type=F32)

    acc_ref[...] = jnp.zeros_like(acc_ref)
    up_project(0, ua_ref)

    def body(i, carry):
        c = 2 * i
        up_project(c + 1, ub_ref)
        down_project(c, ua_ref)
        up_project(c + 2, ua_ref)
        down_project(c + 1, ub_ref)
        return carry

    lax.fori_loop(0, n_chunks // 2, body, 0)
    down_project(n_chunks - 1, ua_ref)
    y = acc_ref[...]

    b = pl.ds(pl.program_id(0), 1)
    if emit_next:
        _residual_and_next_norm(x_ref[...], y, gate_ref[b, :], gpost_ref[...], gnext_ref[...],
                                sh_ref[b, :], sc_ref[b, :], x_out_ref, h_out_ref)
    else:
        _residual_and_next_norm(x_ref[...], y, gate_ref[b, :], gpost_ref[...], None, None, None,
                                x_out_ref, None)


def _conv_ffn(h, x, w_up, w_dw, b_dw, w_down, mod, layer, g_post, g_next=None):
    bsz, s, d = x.shape
    tm = TM_FFN
    emit_next = g_next is not None
    n_blocks = 2 * FFN_DIM // FFN_CHUNK
    w_up = w_up.reshape(d, n_blocks, FFN_CHUNK).transpose(1, 0, 2).astype(BF16)
    w_dw = w_dw.reshape(FFN_CONV_WIDTH, n_blocks, FFN_CHUNK).transpose(1, 0, 2)
    b_dw = b_dw.reshape(n_blocks, 1, FFN_CHUNK)
    w_down = w_down.astype(BF16)
    tile = lambda: pl.BlockSpec((None, tm, d), lambda b, i: (b, i, 0))
    in_specs = [*_halo_specs(tm, d, s), tile(), _resident(w_up.shape), _resident(w_dw.shape),
                _resident(b_dw.shape), _resident(w_down.shape), _mod_spec(layer, 5),
                _resident(g_post.shape)]
    args = [h, h, h, x, w_up, w_dw, b_dw, w_down, mod, g_post]
    out_shape = [jax.ShapeDtypeStruct((bsz, s, d), F32)]
    out_specs = [tile()]
    if emit_next:
        in_specs += [_resident(g_next.shape), _mod_spec(layer + 1, 0), _mod_spec(layer + 1, 1)]
        args += [g_next, mod, mod]
        out_shape.append(jax.ShapeDtypeStruct((bsz, s, d), BF16))
        out_specs.append(tile())
    return pl.pallas_call(
        functools.partial(_ffn_kernel, emit_next=emit_next),
        out_shape=tuple(out_shape),
        grid=(bsz, s // tm),
        in_specs=in_specs,
        out_specs=tuple(out_specs),
        scratch_shapes=[pltpu.VMEM((tm + 2 * HALO, d), BF16),
                        pltpu.VMEM((2, tm + 2 * HALO, FFN_CHUNK), F32),
                        pltpu.VMEM((2, tm + 2 * HALO, FFN_CHUNK), F32),
                        pltpu.VMEM((tm, d), F32)],
        compiler_params=_cparams(2),
        name="conv_ffn",
    )(*args)


def _conformer_kernel(hp_ref, h_ref, hn_ref, x_ref, wpw1_ref, bpw1_ref, wdw_ref, bdw_ref, gln_ref,
                      bln_ref, wpw2_ref, bpw2_ref, gate_ref, sh_ref, sc_ref, gpost_ref, gnext_ref,
                      x_out_ref, h_out_ref, hext_ref, glu_ref, conv_ref):
    i = pl.program_id(1)
    tm = h_ref.shape[0]
    rows = tm + 2 * HALO
    d = D_MODEL
    n_blk = d // LANES
    _gather_halo(hp_ref, h_ref, hn_ref, hext_ref)

    u = jnp.dot(hext_ref[...], wpw1_ref[...], preferred_element_type=F32) + bpw1_ref[...]
    glu = u[:, :d] * jax.nn.sigmoid(u[:, d:])
    r = lax.broadcasted_iota(jnp.int32, (rows, 1), 0)
    first_row = jnp.where(i > 0, 0, HALO)
    end_row = jnp.where(i < pl.num_programs(1) - 1, rows, HALO + tm)
    inside = jnp.logical_and(r >= first_row, r < end_row)
    glu = jnp.where(inside, glu, 0.0)
    for c in range(n_blk):
        glu_ref[c] = glu[:, c * LANES:(c + 1) * LANES]

    pad = (CONV_WIDTH - 1) // 2

    def conv_block(c, carry):
        w = wdw_ref[c]
        for r0 in range(0, tm, CONF_ROWS):
            acc = jnp.broadcast_to(bdw_ref[c], (CONF_ROWS, LANES))
            for k in range(CONV_WIDTH):
                acc = acc + glu_ref[c, pl.ds(HALO - pad + k + r0, CONF_ROWS), :] * w[k:k + 1, :]
            conv_ref[c, r0:r0 + CONF_ROWS, :] = acc
        return carry

    lax.fori_loop(0, n_blk, conv_block, 0)

    v = jnp.concatenate([conv_ref[c] for c in range(n_blk)], axis=-1)
    mu = jnp.mean(v, axis=-1, keepdims=True)
    var = jnp.mean(jnp.square(v - mu), axis=-1, keepdims=True)
    ln = (v - mu) * lax.rsqrt(var + EPS) * gln_ref[...] + bln_ref[...]
    z = (ln * jax.nn.sigmoid(ln)).astype(BF16)
    y = jnp.dot(z, wpw2_ref[...], preferred_element_type=F32) + bpw2_ref[...]

    b = pl.ds(pl.program_id(0), 1)
    _residual_and_next_norm(x_ref[...], y, gate_ref[b, :], gpost_ref[...], gnext_ref[...],
                            sh_ref[b, :], sc_ref[b, :], x_out_ref, h_out_ref)


def _conformer(h, x, w_pw1, b_pw1, w_dw_blk, b_dw_blk, g_ln, b_ln, w_pw2, b_pw2, mod, layer,
               g_post, g_next):
    bsz, s, d = x.shape
    tm = TM_CONF
    n_blk = d // LANES
    tile = lambda: pl.BlockSpec((None, tm, d), lambda b, i: (b, i, 0))
    consts = [w_pw1, b_pw1, w_dw_blk, b_dw_blk, g_ln, b_ln, w_pw2, b_pw2]
    return pl.pallas_call(
        _conformer_kernel,
        out_shape=(jax.ShapeDtypeStruct((bsz, s, d), F32), jax.ShapeDtypeStruct((bsz, s, d), BF16)),
        grid=(bsz, s // tm),
        in_specs=[*_halo_specs(tm, d, s), tile(), *[_resident(a.shape) for a in consts],
                  _mod_spec(layer, 2), _mod_spec(layer, 3), _mod_spec(layer, 4),
                  _resident(g_post.shape), _resident(g_next.shape)],
        out_specs=(tile(), tile()),
        scratch_shapes=[pltpu.VMEM((tm + 2 * HALO, d), BF16),
                        pltpu.VMEM((n_blk, tm + 2 * HALO, LANES), F32),
                        pltpu.VMEM((n_blk, tm, LANES), F32)],
        compiler_params=_cparams(2),
        name="conformer",
    )(h, h, h, x, *consts, mod, mod, mod, g_post, g_next)


def _rope_tables(seq_len):
    n = ROPE_DIM // 4
    pos = np.arange(seq_len)
    inv = np.float32(ROPE_THETA) ** (-np.arange(n, dtype=np.float32) / np.float32(n))
    ang_r = (pos // GRID_W).astype(np.float32)[:, None] * inv
    ang_c = (pos % GRID_W).astype(np.float32)[:, None] * inv
    zeros = np.zeros((seq_len, LANES - ROPE_DIM), np.float32)
    cos = np.concatenate([np.cos(ang_r), np.cos(ang_r), np.cos(ang_c), np.cos(ang_c), zeros], -1)
    sin = np.concatenate([-np.sin(ang_r), np.sin(ang_r), -np.sin(ang_c), np.sin(ang_c), zeros], -1)
    return jnp.asarray(cos, F32), jnp.asarray(sin, F32)


def _swap_halves_perm():
    n = ROPE_DIM // 4
    j = np.arange(ROPE_DIM)
    return (j // (2 * n)) * 2 * n + (1 - (j % (2 * n)) // n) * n + j % n


def _pad_lanes(w):
    return jnp.pad(w, [(0, 0)] * (w.ndim - 1) + [(0, LANES - w.shape[-1])])


def _mla_weights(w_dqkv, w_uq):
    perm = _swap_halves_perm()
    kr = w_dqkv[:, Q_LORA + KV_LORA:]
    wd = jnp.concatenate([w_dqkv[:, :Q_LORA + KV_LORA], _pad_lanes(kr), _pad_lanes(kr[:, perm])], -1)
    wq = w_uq.reshape(Q_LORA, MLA_HEADS, QK_DIM)
    rope = wq[:, :, NOPE_DIM:]
    wq = jnp.concatenate([wq[:, :, :NOPE_DIM].reshape(Q_LORA, -1),
                          _pad_lanes(rope).reshape(Q_LORA, -1),
                          _pad_lanes(rope[:, :, perm]).reshape(Q_LORA, -1)], -1)
    return wd.astype(BF16), wq.astype(BF16)


def kernel(x, c, ctx, c_ctx, w_mod, b_mod, g_pre_mix, g_post_mix, g_pre_ffn, g_post_ffn, mla_w_dqkv, mla_g_q, mla_w_uq, mla_g_kv, mla_w_ukv, mla_w_o, cv_w_pw1, cv_b_pw1, cv_w_dw, cv_b_dw, cv_g_ln, cv_b_ln, cv_w_pw2, cv_b_pw2, ffn_w_up, ffn_w_dw, ffn_b_dw, ffn_w_down):
    bsz, s, d = x.shape
    assert (bsz, d) == (c.shape[0], D_MODEL) and bsz + 1 <= MOD_ROWS and DEPTH == 2
    row = lambda v: v.reshape(1, -1)

    cc = jnp.concatenate([c, c_ctx[None, :], jnp.zeros((MOD_ROWS - bsz - 1, d), F32)], axis=0)
    mod = _adaln_mod(cc, w_mod, b_mod)

    cos, sin = _rope_tables(s)
    wd, wq = _mla_weights(mla_w_dqkv[0], mla_w_uq[0])
    wukv = mla_w_ukv[0].astype(BF16)
    g0 = row(g_pre_mix[0])
    g_kv = row(mla_g_kv[0])
    q, k_lat, v_lat = _mla_project_latent(x, mod, g0, wd, row(mla_g_q[0]), wq, g_kv, wukv, cos, sin)
    k_ctx, v_ctx = _mla_project_context(ctx, mod, g0, wd[:, Q_LORA:Q_LORA + KV_LORA + LANES], g_kv,
                                        wukv, bsz)
    o = _mla_attention(q, k_ctx, k_lat, v_ctx, v_lat)
    x1, h = _attn_post(o, x, mla_w_o[0].astype(BF16), mod, 0, row(g_post_mix[0]), row(g_pre_ffn[0]))
    x2, h = _conv_ffn(h, x1, ffn_w_up[0], ffn_w_dw[0], ffn_b_dw[0], ffn_w_down[0], mod, 0,
                      row(g_post_ffn[0]), row(g_pre_mix[1]))

    n_blk = d // LANES
    w_dw_blk = jnp.pad(cv_w_dw[0], ((0, 1), (0, 0))).reshape(CONV_WIDTH + 1, n_blk, LANES)
    w_dw_blk = w_dw_blk.transpose(1, 0, 2)
    b_dw_blk = cv_b_dw[0].reshape(n_blk, 1, LANES)
    x3, h = _conformer(h, x2, cv_w_pw1[0].astype(BF16), row(cv_b_pw1[0]), w_dw_blk, b_dw_blk,
                       row(cv_g_ln[0]), row(cv_b_ln[0]), cv_w_pw2[0].astype(BF16), row(cv_b_pw2[0]),
                       mod, 1, row(g_post_mix[1]), row(g_pre_ffn[1]))
    (x4,) = _conv_ffn(h, x3, ffn_w_up[1], ffn_w_dw[1], ffn_b_dw[1], ffn_w_down[1], mod, 1,
                      row(g_post_ffn[1]))
    return x4
```

```python
import functools
import math

import jax
import jax.numpy as jnp
import numpy as np
from jax import lax
from jax.experimental import pallas as pl
from jax.experimental.pallas import tpu as pltpu

F32 = jnp.float32
BF16 = jnp.bfloat16

D_MODEL = 1024
DEPTH = 2
GRID_W = 64
MLA_HEADS = 8
Q_LORA = 384
KV_LORA = 128
NOPE_DIM = 128
ROPE_DIM = 64
V_DIM = 128
QK_DIM = NOPE_DIM + ROPE_DIM
ROPE_THETA = 10000.0
CONV_WIDTH = 31
FFN_DIM = 2816
FFN_CONV_WIDTH = 3
EPS = 1e-6

LANES = 128
QK_PAD = 2 * LANES
V_PAD = 2 * V_DIM
MOD_ROWS = 8
HALO = 16
VMEM_LIMIT = 56 * 1024 * 1024

TM_PROJ = 512
TQ_ATTN = 1024
TK_ATTN = 2048
TM_POST = 512
TM_FFN = 512
FFN_CHUNK = 256
TM_CONF = 512
CONF_ROWS = 128


def _rms(x):
    return x * lax.rsqrt(jnp.mean(x * x, axis=-1, keepdims=True) + EPS)


def _modulated_norm(x, g, shift, scale):
    return (_rms(x) * g) * (1.0 + scale) + shift


def _cparams(n_axes):
    return pltpu.CompilerParams(
        dimension_semantics=("parallel",) * n_axes, vmem_limit_bytes=VMEM_LIMIT)


def _resident(shape):
    zeros = (0,) * len(shape)
    return pl.BlockSpec(shape, lambda *_: zeros, pipeline_mode=pl.Buffered(1))


def _mod_kernel(cc_ref, w_ref, b_ref, o_ref):
    a = cc_ref[...]
    a = a * jax.nn.sigmoid(a)
    o_ref[...] = jnp.dot(a.astype(BF16), w_ref[...].astype(BF16),
                         preferred_element_type=F32) + b_ref[...]


def _adaln_mod(cc, w_mod, b_mod):
    d = D_MODEL
    return pl.pallas_call(
        _mod_kernel,
        out_shape=jax.ShapeDtypeStruct((DEPTH, 6, MOD_ROWS, d), F32),
        grid=(DEPTH, 6),
        in_specs=[
            pl.BlockSpec((MOD_ROWS, d), lambda i, j: (0, 0)),
            pl.BlockSpec((None, d, d), lambda i, j: (i, 0, j)),
            pl.BlockSpec((None, 1, d), lambda i, j: (i, 0, j)),
        ],
        out_specs=pl.BlockSpec((None, None, MOD_ROWS, d), lambda i, j: (i, j, 0, 0)),
        compiler_params=_cparams(2),
        name="adaln_mod",
    )(cc, w_mod, b_mod.reshape(DEPTH, 1, 6 * d))


def _mod_spec(layer, chunk):
    return pl.BlockSpec((None, None, MOD_ROWS, D_MODEL), lambda *_: (layer, chunk, 0, 0))


def _kv_project(h, wd_kv, g_kv, w_ukv):
    dkv = jnp.dot(h, wd_kv, preferred_element_type=F32)
    nkv = (_rms(dkv[:, :KV_LORA]) * g_kv).astype(BF16)
    kv = jnp.dot(nkv, w_ukv, preferred_element_type=F32)
    return kv, dkv[:, KV_LORA:]


def _store_kv(kv, kr, k_ref, v_ref):
    for hh in range(MLA_HEADS):
        base = hh * (NOPE_DIM + V_DIM)
        k_ref[hh, :, 0:NOPE_DIM] = kv[:, base:base + NOPE_DIM].astype(BF16)
        k_ref[hh, :, NOPE_DIM:QK_PAD] = kr
        v_ref[hh, :, 0:V_DIM] = kv[:, base + NOPE_DIM:base + NOPE_DIM + V_DIM].astype(BF16)
        v_ref[hh, :, V_DIM:] = jnp.ones((kv.shape[0], V_DIM), BF16)


def _mla_lat_kernel(x_ref, sh_ref, sc_ref, g_ref, wd_ref, gq_ref, wq_ref, gkv_ref, wukv_ref,
                    cos_ref, sin_ref, q_ref, k_ref, v_ref):
    b = pl.program_id(0)
    shift = sh_ref[pl.ds(b, 1), :]
    scale = sc_ref[pl.ds(b, 1), :]
    h = _modulated_norm(x_ref[...], g_ref[...], shift, scale).astype(BF16)
    cos = cos_ref[...]
    sin = sin_ref[...]

    kv, kr_raw = _kv_project(h, wd_ref[:, Q_LORA:], gkv_ref[...], wukv_ref[...])
    kr = (kr_raw[:, :LANES] * cos + kr_raw[:, LANES:] * sin).astype(BF16)
    _store_kv(kv, kr, k_ref, v_ref)

    cq = jnp.dot(h, wd_ref[:, :Q_LORA], preferred_element_type=F32)
    nq = (_rms(cq) * (gq_ref[...] * (math.log2(math.e) / math.sqrt(QK_DIM)))).astype(BF16)
    n_nope = MLA_HEADS * NOPE_DIM
    n_rope = MLA_HEADS * LANES
    q_nope = jnp.dot(nq, wq_ref[:, :n_nope], preferred_element_type=F32)
    q_rope = jnp.dot(nq, wq_ref[:, n_nope:n_nope + n_rope], preferred_element_type=F32)
    q_swap = jnp.dot(nq, wq_ref[:, n_nope + n_rope:], preferred_element_type=F32)
    for hh in range(MLA_HEADS):
        q_ref[hh, :, 0:NOPE_DIM] = q_nope[:, hh * NOPE_DIM:(hh + 1) * NOPE_DIM].astype(BF16)
        cols = slice(hh * LANES, (hh + 1) * LANES)
        q_ref[hh, :, NOPE_DIM:QK_PAD] = (q_rope[:, cols] * cos + q_swap[:, cols] * sin).astype(BF16)


def _mla_ctx_kernel(x_ref, sh_ref, sc_ref, g_ref, wd_ref, gkv_ref, wukv_ref, k_ref, v_ref, *, row):
    shift = sh_ref[row:row + 1, :]
    scale = sc_ref[row:row + 1, :]
    h = _modulated_norm(x_ref[...], g_ref[...], shift, scale).astype(BF16)
    kv, kr_raw = _kv_project(h, wd_ref[...], gkv_ref[...], wukv_ref[...])
    _store_kv(kv, kr_raw[:, :LANES].astype(BF16), k_ref, v_ref)


def _mla_project_latent(x, mod, g_pre, wd, g_q, wq, g_kv, wukv, cos, sin):
    bsz, s, d = x.shape
    tm = TM_PROJ
    hd = MLA_HEADS
    out_shape = (
        jax.ShapeDtypeStruct((bsz, hd, s, QK_PAD), BF16),
        jax.ShapeDtypeStruct((bsz, hd, s, QK_PAD), BF16),
        jax.ShapeDtypeStruct((bsz, hd, s, V_PAD), BF16),
    )
    head_spec = lambda w: pl.BlockSpec((None, hd, tm, w), lambda b, i: (b, 0, i, 0))
    return pl.pallas_call(
        _mla_lat_kernel,
        out_shape=out_shape,
        grid=(bsz, s // tm),
        in_specs=[
            pl.BlockSpec((None, tm, d), lambda b, i: (b, i, 0)),
            _mod_spec(0, 0), _mod_spec(0, 1),
            _resident(g_pre.shape), _resident(wd.shape), _resident(g_q.shape), _resident(wq.shape),
            _resident(g_kv.shape), _resident(wukv.shape),
            pl.BlockSpec((tm, LANES), lambda b, i: (i, 0)),
            pl.BlockSpec((tm, LANES), lambda b, i: (i, 0)),
        ],
        out_specs=(head_spec(QK_PAD), head_spec(QK_PAD), head_spec(V_PAD)),
        compiler_params=_cparams(2),
        name="mla_proj_latent",
    )(x, mod, mod, g_pre, wd, g_q, wq, g_kv, wukv, cos, sin)


def _mla_project_context(ctx, mod, g_pre, wd_kv, g_kv, wukv, row):
    bsz, c, d = ctx.shape
    hd = MLA_HEADS
    out_shape = (
        jax.ShapeDtypeStruct((bsz, hd, c, QK_PAD), BF16),
        jax.ShapeDtypeStruct((bsz, hd, c, V_PAD), BF16),
    )
    head_spec = lambda w: pl.BlockSpec((None, hd, c, w), lambda b: (b, 0, 0, 0))
    return pl.pallas_call(
        functools.partial(_mla_ctx_kernel, row=row),
        out_shape=out_shape,
        grid=(bsz,),
        in_specs=[
            pl.BlockSpec((None, c, d), lambda b: (b, 0, 0)),
            _mod_spec(0, 0), _mod_spec(0, 1),
            _resident(g_pre.shape), _resident(wd_kv.shape), _resident(g_kv.shape),
            _resident(wukv.shape),
        ],
        out_specs=(head_spec(QK_PAD), head_spec(V_PAD)),
        compiler_params=_cparams(1),
        name="mla_proj_context",
    )(ctx, mod, mod, g_pre, wd_kv, g_kv, wukv)


def _attn_kernel(q_ref, kc_ref, kl_ref, vc_ref, vl_ref, o_ref):
    q = q_ref[...]
    tq = q.shape[0]

    def scores(k):
        return lax.dot_general(q, k, (((1,), (1,)), ((), ())), preferred_element_type=F32)

    def accumulate(s, v, carry):
        m, acc = carry
        m_new = jnp.maximum(m, jnp.max(s, axis=-1, keepdims=True))
        p = jnp.exp2(s - m_new)
        alpha = jnp.exp2(m - m_new)
        acc = alpha * acc + jnp.dot(p.astype(BF16), v, preferred_element_type=F32)
        return m_new, acc

    n_lat = kl_ref.shape[0] // TK_ATTN
    chunk = lambda j: slice(j * TK_ATTN, (j + 1) * TK_ATTN)
    carry = (jnp.full((tq, 1), -jnp.inf, F32), jnp.zeros((tq, 2 * V_DIM), F32))
    s = scores(kc_ref[...])
    v = vc_ref[...]
    for j in range(n_lat):
        s_next = scores(kl_ref[chunk(j), :])
        carry = accumulate(s, v, carry)
        s, v = s_next, vl_ref[chunk(j), :]
    _, acc = accumulate(s, v, carry)
    o_ref[...] = (acc[:, :V_DIM] / acc[:, V_DIM:]).astype(o_ref.dtype)


def _mla_attention(q, k_ctx, k_lat, v_ctx, v_lat):
    bsz, hd, s, _ = q.shape
    c = k_ctx.shape[2]
    tq = TQ_ATTN
    per_head = lambda rows, w: pl.BlockSpec((None, None, rows, w), lambda b, h, i: (b, h, 0, 0))
    return pl.pallas_call(
        _attn_kernel,
        out_shape=jax.ShapeDtypeStruct((bsz, s, hd * V_DIM), BF16),
        grid=(bsz, hd, s // tq),
        in_specs=[
            pl.BlockSpec((None, None, tq, QK_PAD), lambda b, h, i: (b, h, i, 0)),
            per_head(c, QK_PAD), per_head(s, QK_PAD), per_head(c, V_PAD), per_head(s, V_PAD),
        ],
        out_specs=pl.BlockSpec((None, tq, V_DIM), lambda b, h, i: (b, i, h)),
        compiler_params=_cparams(3),
        name="mla_attn",
    )(q, k_ctx, k_lat, v_ctx, v_lat)


def _residual_and_next_norm(x, y, gate, g_post, g_next, shift, scale, x_out_ref, h_out_ref):
    x_new = x + gate * (_rms(y) * g_post)
    x_out_ref[...] = x_new
    if h_out_ref is not None:
        h_out_ref[...] = _modulated_norm(x_new, g_next, shift, scale).astype(BF16)


def _attn_post_kernel(o_ref, x_ref, wo_ref, gate_ref, sh_ref, sc_ref, gpost_ref, gnext_ref,
                      x_out_ref, h_out_ref):
    b = pl.ds(pl.program_id(0), 1)
    y = jnp.dot(o_ref[...], wo_ref[...], preferred_element_type=F32)
    _residual_and_next_norm(x_ref[...], y, gate_ref[b, :], gpost_ref[...], gnext_ref[...],
                            sh_ref[b, :], sc_ref[b, :], x_out_ref, h_out_ref)


def _attn_post(o, x, w_o, mod, layer, g_post, g_next):
    bsz, s, d = x.shape
    tm = TM_POST
    tile = lambda: pl.BlockSpec((None, tm, d), lambda b, i: (b, i, 0))
    return pl.pallas_call(
        _attn_post_kernel,
        out_shape=(jax.ShapeDtypeStruct((bsz, s, d), F32), jax.ShapeDtypeStruct((bsz, s, d), BF16)),
        grid=(bsz, s // tm),
        in_specs=[tile(), tile(), _resident(w_o.shape),
                  _mod_spec(layer, 2), _mod_spec(layer, 3), _mod_spec(layer, 4),
                  _resident(g_post.shape), _resident(g_next.shape)],
        out_specs=(tile(), tile()),
        compiler_params=_cparams(2),
        name="attn_post",
    )(o, x, w_o, mod, mod, mod, g_post, g_next)


def _halo_specs(tm, d, s):
    per = tm // HALO
    last = s // HALO - 1
    main = pl.BlockSpec((None, tm, d), lambda b, i: (b, i, 0))
    prev = pl.BlockSpec((None, HALO, d), lambda b, i: (b, jnp.maximum(i * per - 1, 0), 0))
    nxt = pl.BlockSpec((None, HALO, d), lambda b, i: (b, jnp.minimum((i + 1) * per, last), 0))
    return prev, main, nxt


def _gather_halo(hp_ref, h_ref, hn_ref, hext_ref):
    i = pl.program_id(1)
    tm = h_ref.shape[0]
    hp = hp_ref[...]
    hn = hn_ref[...]
    has_prev = jnp.broadcast_to((i > 0).astype(jnp.int32), hp.shape) > 0
    has_next = jnp.broadcast_to((i < pl.num_programs(1) - 1).astype(jnp.int32), hn.shape) > 0
    hext_ref[0:HALO, :] = jnp.where(has_prev, hp, jnp.zeros_like(hp))
    hext_ref[HALO:HALO + tm, :] = h_ref[...]
    hext_ref[HALO + tm:, :] = jnp.where(has_next, hn, jnp.zeros_like(hn))


def _ffn_kernel(*refs, emit_next):
    (hp_ref, h_ref, hn_ref, x_ref, wup_ref, wdw_ref, bdw_ref, wdn_ref, gate_ref, gpost_ref) = refs[:10]
    if emit_next:
        gnext_ref, sh_ref, sc_ref, x_out_ref, h_out_ref, hext_ref, ua_ref, ub_ref, acc_ref = refs[10:]
    else:
        (x_out_ref, hext_ref, ua_ref, ub_ref, acc_ref), h_out_ref = refs[10:], None
    tm = h_ref.shape[0]
    n_chunks = FFN_DIM // FFN_CHUNK
    assert n_chunks % 2 == 1
    _gather_halo(hp_ref, h_ref, hn_ref, hext_ref)

    def up_project(c, u_ref):
        hext = hext_ref[...]
        cols = lambda blk: pl.ds(pl.multiple_of(blk * FFN_CHUNK, FFN_CHUNK), FFN_CHUNK)
        u_ref[0] = jnp.dot(hext, wup_ref[:, cols(c)], preferred_element_type=F32)
        u_ref[1] = jnp.dot(hext, wup_ref[:, cols(n_chunks + c)], preferred_element_type=F32)

    def conv3(c, u_ref, j):
        w = wdw_ref[j * n_chunks + c]
        return (u_ref[j, pl.ds(HALO - 1, tm), :] * w[0:1]
                + u_ref[j, pl.ds(HALO, tm), :] * w[1:2]
                + u_ref[j, pl.ds(HALO + 1, tm), :] * w[2:3]) + bdw_ref[j * n_chunks + c]

    def down_project(c, u_ref):
        gate = conv3(c, u_ref, 0)
        val = conv3(c, u_ref, 1)
        z = (gate * jax.nn.sigmoid(gate) * val).astype(BF16)
        rows = pl.ds(pl.multiple_of(c * FFN_CHUNK, FFN_CHUNK), FFN_CHUNK)
        acc_ref[...] += jnp.dot(z, wdn_ref[rows, :], preferred_element_type=F32)

    acc_ref[...] = jnp.zeros_like(acc_ref)
    up_project(0, ua_ref)

    def body(i, carry):
        c = 2 * i
        up_project(c + 1, ub_ref)
        down_project(c, ua_ref)
        up_project(c + 2, ua_ref)
        down_project(c + 1, ub_ref)
        return carry

    lax.fori_loop(0, n_chunks // 2, body, 0)
    down_project(n_chunks - 1, ua_ref)
    y = acc_ref[...]

    b = pl.ds(pl.program_id(0), 1)
    if emit_next:
        _residual_and_next_norm(x_ref[...], y, gate_ref[b, :], gpost_ref[...], gnext_ref[...],
                                sh_ref[b, :], sc_ref[b, :], x_out_ref, h_out_ref)
    else:
        _residual_and_next_norm(x_ref[...], y, gate_ref[b, :], gpost_ref[...], None, None, None,
                                x_out_ref, None)


def _conv_ffn(h, x, w_up, w_dw, b_dw, w_down, mod, layer, g_post, g_next=None):
    bsz, s, d = x.shape
    tm = TM_FFN
    emit_next = g_next is not None
    n_blocks = 2 * FFN_DIM // FFN_CHUNK
    w_up = w_up.astype(BF16)
    w_dw = w_dw.reshape(FFN_CONV_WIDTH, n_blocks, FFN_CHUNK).transpose(1, 0, 2)
    b_dw = b_dw.reshape(n_blocks, 1, FFN_CHUNK)
    w_down = w_down.astype(BF16)
    tile = lambda: pl.BlockSpec((None, tm, d), lambda b, i: (b, i, 0))
    in_specs = [*_halo_specs(tm, d, s), tile(), _resident(w_up.shape), _resident(w_dw.shape),
                _resident(b_dw.shape), _resident(w_down.shape), _mod_spec(layer, 5),
                _resident(g_post.shape)]
    args = [h, h, h, x, w_up, w_dw, b_dw, w_down, mod, g_post]
    out_shape = [jax.ShapeDtypeStruct((bsz, s, d), F32)]
    out_specs = [tile()]
    if emit_next:
        in_specs += [_resident(g_next.shape), _mod_spec(layer + 1, 0), _mod_spec(layer + 1, 1)]
        args += [g_next, mod, mod]
        out_shape.append(jax.ShapeDtypeStruct((bsz, s, d), BF16))
        out_specs.append(tile())
    return pl.pallas_call(
        functools.partial(_ffn_kernel, emit_next=emit_next),
        out_shape=tuple(out_shape),
        grid=(bsz, s // tm),
        in_specs=in_specs,
        out_specs=tuple(out_specs),
        scratch_shapes=[pltpu.VMEM((tm + 2 * HALO, d), BF16),
                        pltpu.VMEM((2, tm + 2 * HALO, FFN_CHUNK), F32),
                        pltpu.VMEM((2, tm + 2 * HALO, FFN_CHUNK), F32),
                        pltpu.VMEM((tm, d), F32)],
        compiler_params=_cparams(2),
        name="conv_ffn",
    )(*args)


def _conformer_kernel(hp_ref, h_ref, hn_ref, x_ref, wpw1_ref, bpw1_ref, wdw_ref, bdw_ref, gln_ref,
                      bln_ref, wpw2_ref, bpw2_ref, gate_ref, sh_ref, sc_ref, gpost_ref, gnext_ref,
                      x_out_ref, h_out_ref, hext_ref, glu_ref, conv_ref):
    i = pl.program_id(1)
    tm = h_ref.shape[0]
    rows = tm + 2 * HALO
    d = D_MODEL
    n_blk = d // LANES
    _gather_halo(hp_ref, h_ref, hn_ref, hext_ref)

    u = jnp.dot(hext_ref[...], wpw1_ref[...], preferred_element_type=F32) + bpw1_ref[...]
    glu = u[:, :d] * jax.nn.sigmoid(u[:, d:])
    r = lax.broadcasted_iota(jnp.int32, (rows, 1), 0)
    first_row = jnp.where(i > 0, 0, HALO)
    end_row = jnp.where(i < pl.num_programs(1) - 1, rows, HALO + tm)
    inside = jnp.logical_and(r >= first_row, r < end_row)
    glu = jnp.where(inside, glu, 0.0)
    for c in range(n_blk):
        glu_ref[c] = glu[:, c * LANES:(c + 1) * LANES]

    pad = (CONV_WIDTH - 1) // 2

    def conv_block(c, carry):
        w = wdw_ref[c]
        for r0 in range(0, tm, CONF_ROWS):
            acc = jnp.broadcast_to(bdw_ref[c], (CONF_ROWS, LANES))
            for k in range(CONV_WIDTH):
                acc = acc + glu_ref[c, pl.ds(HALO - pad + k + r0, CONF_ROWS), :] * w[k:k + 1, :]
            conv_ref[c, r0:r0 + CONF_ROWS, :] = acc
        return carry

    lax.fori_loop(0, n_blk, conv_block, 0)

    v = jnp.concatenate([conv_ref[c] for c in range(n_blk)], axis=-1)
    mu = jnp.mean(v, axis=-1, keepdims=True)
    var = jnp.mean(jnp.square(v - mu), axis=-1, keepdims=True)
    ln = (v - mu) * lax.rsqrt(var + EPS) * gln_ref[...] + bln_ref[...]
    z = (ln * jax.nn.sigmoid(ln)).astype(BF16)
    y = jnp.dot(z, wpw2_ref[...], preferred_element_type=F32) + bpw2_ref[...]

    b = pl.ds(pl.program_id(0), 1)
    _residual_and_next_norm(x_ref[...], y, gate_ref[b, :], gpost_ref[...], gnext_ref[...],
                            sh_ref[b, :], sc_ref[b, :], x_out_ref, h_out_ref)


def _conformer(h, x, w_pw1, b_pw1, w_dw_blk, b_dw_blk, g_ln, b_ln, w_pw2, b_pw2, mod, layer,
               g_post, g_next):
    bsz, s, d = x.shape
    tm = TM_CONF
    n_blk = d // LANES
    tile = lambda: pl.BlockSpec((None, tm, d), lambda b, i: (b, i, 0))
    consts = [w_pw1, b_pw1, w_dw_blk, b_dw_blk, g_ln, b_ln, w_pw2, b_pw2]
    return pl.pallas_call(
        _conformer_kernel,
        out_shape=(jax.ShapeDtypeStruct((bsz, s, d), F32), jax.ShapeDtypeStruct((bsz, s, d), BF16)),
        grid=(bsz, s // tm),
        in_specs=[*_halo_specs(tm, d, s), tile(), *[_resident(a.shape) for a in consts],
                  _mod_spec(layer, 2), _mod_spec(layer, 3), _mod_spec(layer, 4),
                  _resident(g_post.shape), _resident(g_next.shape)],
        out_specs=(tile(), tile()),
        scratch_shapes=[pltpu.VMEM((tm + 2 * HALO, d), BF16),
                        pltpu.VMEM((n_blk, tm + 2 * HALO, LANES), F32),
                        pltpu.VMEM((n_blk, tm, LANES), F32)],
        compiler_params=_cparams(2),
        name="conformer",
    )(h, h, h, x, *consts, mod, mod, mod, g_post, g_next)


def _rope_tables(seq_len):
    n = ROPE_DIM // 4
    pos = np.arange(seq_len)
    inv = np.float32(ROPE_THETA) ** (-np.arange(n, dtype=np.float32) / np.float32(n))
    ang_r = (pos // GRID_W).astype(np.float32)[:, None] * inv
    ang_c = (pos % GRID_W).astype(np.float32)[:, None] * inv
    zeros = np.zeros((seq_len, LANES - ROPE_DIM), np.float32)
    cos = np.concatenate([np.cos(ang_r), np.cos(ang_r), np.cos(ang_c), np.cos(ang_c), zeros], -1)
    sin = np.concatenate([-np.sin(ang_r), np.sin(ang_r), -np.sin(ang_c), np.sin(ang_c), zeros], -1)
    return jnp.asarray(cos, F32), jnp.asarray(sin, F32)


def _swap_halves_perm():
    n = ROPE_DIM // 4
    j = np.arange(ROPE_DIM)
    return (j // (2 * n)) * 2 * n + (1 - (j % (2 * n)) // n) * n + j % n


def _pad_lanes(w):
    return jnp.pad(w, [(0, 0)] * (w.ndim - 1) + [(0, LANES - w.shape[-1])])


def _mla_weights(w_dqkv, w_uq):
    perm = _swap_halves_perm()
    kr = w_dqkv[:, Q_LORA + KV_LORA:]
    wd = jnp.concatenate([w_dqkv[:, :Q_LORA + KV_LORA], _pad_lanes(kr), _pad_lanes(kr[:, perm])], -1)
    wq = w_uq.reshape(Q_LORA, MLA_HEADS, QK_DIM)
    rope = wq[:, :, NOPE_DIM:]
    wq = jnp.concatenate([wq[:, :, :NOPE_DIM].reshape(Q_LORA, -1),
                          _pad_lanes(rope).reshape(Q_LORA, -1),
                          _pad_lanes(rope[:, :, perm]).reshape(Q_LORA, -1)], -1)
    return wd.astype(BF16), wq.astype(BF16)


def kernel(x, c, ctx, c_ctx, w_mod, b_mod, g_pre_mix, g_post_mix, g_pre_ffn, g_post_ffn, mla_w_dqkv, mla_g_q, mla_w_uq, mla_g_kv, mla_w_ukv, mla_w_o, cv_w_pw1, cv_b_pw1, cv_w_dw, cv_b_dw, cv_g_ln, cv_b_ln, cv_w_pw2, cv_b_pw2, ffn_w_up, ffn_w_dw, ffn_b_dw, ffn_w_down):
    bsz, s, d = x.shape
    assert (bsz, d) == (c.shape[0], D_MODEL) and bsz + 1 <= MOD_ROWS and DEPTH == 2
    row = lambda v: v.reshape(1, -1)

    cc = jnp.concatenate([c, c_ctx[None, :], jnp.zeros((MOD_ROWS - bsz - 1, d), F32)], axis=0)
    mod = _adaln_mod(cc, w_mod, b_mod)

    cos, sin = _rope_tables(s)
    wd, wq = _mla_weights(mla_w_dqkv[0], mla_w_uq[0])
    wukv = mla_w_ukv[0].astype(BF16)
    g0 = row(g_pre_mix[0])
    g_kv = row(mla_g_kv[0])
    q, k_lat, v_lat = _mla_project_latent(x, mod, g0, wd, row(mla_g_q[0]), wq, g_kv, wukv, cos, sin)
    k_ctx, v_ctx = _mla_project_context(ctx, mod, g0, wd[:, Q_LORA:Q_LORA + KV_LORA + LANES], g_kv,
                                        wukv, bsz)
    o = _mla_attention(q, k_ctx, k_lat, v_ctx, v_lat)
    x1, h = _attn_post(o, x, mla_w_o[0].astype(BF16), mod, 0, row(g_post_mix[0]), row(g_pre_ffn[0]))
    x2, h = _conv_ffn(h, x1, ffn_w_up[0], ffn_w_dw[0], ffn_b_dw[0], ffn_w_down[0], mod, 0,
                      row(g_post_ffn[0]), row(g_pre_mix[1]))

    n_blk = d // LANES
    w_dw_blk = jnp.pad(cv_w_dw[0], ((0, 1), (0, 0))).reshape(CONV_WIDTH + 1, n_blk, LANES)
    w_dw_blk = w_dw_blk.transpose(1, 0, 2)
    b_dw_blk = cv_b_dw[0].reshape(n_blk, 1, LANES)
    x3, h = _conformer(h, x2, cv_w_pw1[0].astype(BF16), row(cv_b_pw1[0]), w_dw_blk, b_dw_blk,
                       row(cv_g_ln[0]), row(cv_b_ln[0]), cv_w_pw2[0].astype(BF16), row(cv_b_pw2[0]),
                       mod, 1, row(g_post_mix[1]), row(g_pre_ffn[1]))
    (x4,) = _conv_ffn(h, x3, ffn_w_up[1], ffn_w_dw[1], ffn_b_dw[1], ffn_w_down[1], mod, 1,
                      row(g_post_ffn[1]))
    return x4
```

```python
import functools
import math

import jax
import jax.numpy as jnp
import numpy as np
from jax import lax
from jax.experimental import pallas as pl
from jax.experimental.pallas import tpu as pltpu

F32 = jnp.float32
BF16 = jnp.bfloat16

D_MODEL = 1024
DEPTH = 2
GRID_W = 64
MLA_HEADS = 8
Q_LORA = 384
KV_LORA = 128
NOPE_DIM = 128
ROPE_DIM = 64
V_DIM = 128
QK_DIM = NOPE_DIM + ROPE_DIM
ROPE_THETA = 10000.0
CONV_WIDTH = 31
FFN_DIM = 2816
FFN_CONV_WIDTH = 3
EPS = 1e-6

LANES = 128
QK_PAD = 2 * LANES
V_PAD = 2 * V_DIM
MOD_ROWS = 8
HALO = 16
VMEM_LIMIT = 56 * 1024 * 1024

TM_PROJ = 512
TQ_ATTN = 1024
TK_ATTN = 2048
TM_POST = 512
TM_FFN = 512
FFN_CHUNK = 256
TM_CONF = 512
CONF_ROWS = 128


def _rms(x):
    return x * lax.rsqrt(jnp.mean(x * x, axis=-1, keepdims=True) + EPS)


def _modulated_norm(x, g, shift, scale):
    return (_rms(x) * g) * (1.0 + scale) + shift


def _cparams(n_axes):
    return pltpu.CompilerParams(
        dimension_semantics=("parallel",) * n_axes, vmem_limit_bytes=VMEM_LIMIT)


def _resident(shape):
    zeros = (0,) * len(shape)
    return pl.BlockSpec(shape, lambda *_: zeros, pipeline_mode=pl.Buffered(1))


def _mod_kernel(cc_ref, w_ref, b_ref, o_ref):
    a = cc_ref[...]
    a = a * jax.nn.sigmoid(a)
    o_ref[...] = jnp.dot(a.astype(BF16), w_ref[...].astype(BF16),
                         preferred_element_type=F32) + b_ref[...]


def _adaln_mod(cc, w_mod, b_mod):
    d = D_MODEL
    return pl.pallas_call(
        _mod_kernel,
        out_shape=jax.ShapeDtypeStruct((DEPTH, 6, MOD_ROWS, d), F32),
        grid=(DEPTH, 6),
        in_specs=[
            pl.BlockSpec((MOD_ROWS, d), lambda i, j: (0, 0)),
            pl.BlockSpec((None, d, d), lambda i, j: (i, 0, j)),
            pl.BlockSpec((None, 1, d), lambda i, j: (i, 0, j)),
        ],
        out_specs=pl.BlockSpec((None, None, MOD_ROWS, d), lambda i, j: (i, j, 0, 0)),
        compiler_params=_cparams(2),
        name="adaln_mod",
    )(cc, w_mod, b_mod.reshape(DEPTH, 1, 6 * d))


def _mod_spec(layer, chunk):
    return pl.BlockSpec((None, None, MOD_ROWS, D_MODEL), lambda *_: (layer, chunk, 0, 0))


def _kv_project(h, wd_kv, g_kv, w_ukv):
    dkv = jnp.dot(h, wd_kv, preferred_element_type=F32)
    nkv = (_rms(dkv[:, :KV_LORA]) * g_kv).astype(BF16)
    kv = jnp.dot(nkv, w_ukv, preferred_element_type=F32)
    return kv, dkv[:, KV_LORA:]


def _store_kv(kv, kr, k_ref, v_ref):
    for hh in range(MLA_HEADS):
        base = hh * (NOPE_DIM + V_DIM)
        k_ref[hh, :, 0:NOPE_DIM] = kv[:, base:base + NOPE_DIM].astype(BF16)
        k_ref[hh, :, NOPE_DIM:QK_PAD] = kr
        v_ref[hh, :, 0:V_DIM] = kv[:, base + NOPE_DIM:base + NOPE_DIM + V_DIM].astype(BF16)
        v_ref[hh, :, V_DIM:] = jnp.ones((kv.shape[0], V_DIM), BF16)


def _mla_lat_kernel(x_ref, sh_ref, sc_ref, g_ref, wd_ref, gq_ref, wq_ref, gkv_ref, wukv_ref,
                    cos_ref, sin_ref, q_ref, k_ref, v_ref):
    b = pl.program_id(0)
    shift = sh_ref[pl.ds(b, 1), :]
    scale = sc_ref[pl.ds(b, 1), :]
    h = _modulated_norm(x_ref[...], g_ref[...], shift, scale).astype(BF16)
    cos = cos_ref[...]
    sin = sin_ref[...]

    kv, kr_raw = _kv_project(h, wd_ref[:, Q_LORA:], gkv_ref[...], wukv_ref[...])
    kr = (kr_raw[:, :LANES] * cos + kr_raw[:, LANES:] * sin).astype(BF16)
    _store_kv(kv, kr, k_ref, v_ref)

    cq = jnp.dot(h, wd_ref[:, :Q_LORA], preferred_element_type=F32)
    nq = (_rms(cq) * (gq_ref[...] * (math.log2(math.e) / math.sqrt(QK_DIM)))).astype(BF16)
    n_nope = MLA_HEADS * NOPE_DIM
    n_rope = MLA_HEADS * LANES
    q_nope = jnp.dot(nq, wq_ref[:, :n_nope], preferred_element_type=F32)
    q_rope = jnp.dot(nq, wq_ref[:, n_nope:n_nope + n_rope], preferred_element_type=F32)
    q_swap = jnp.dot(nq, wq_ref[:, n_nope + n_rope:], preferred_element_type=F32)
    for hh in range(MLA_HEADS):
        q_ref[hh, :, 0:NOPE_DIM] = q_nope[:, hh * NOPE_DIM:(hh + 1) * NOPE_DIM].astype(BF16)
        cols = slice(hh * LANES, (hh + 1) * LANES)
        q_ref[hh, :, NOPE_DIM:QK_PAD] = (q_rope[:, cols] * cos + q_swap[:, cols] * sin).astype(BF16)


def _mla_ctx_kernel(x_ref, sh_ref, sc_ref, g_ref, wd_ref, gkv_ref, wukv_ref, k_ref, v_ref, *, row):
    shift = sh_ref[row:row + 1, :]
    scale = sc_ref[row:row + 1, :]
    h = _modulated_norm(x_ref[...], g_ref[...], shift, scale).astype(BF16)
    kv, kr_raw = _kv_project(h, wd_ref[...], gkv_ref[...], wukv_ref[...])
    _store_kv(kv, kr_raw[:, :LANES].astype(BF16), k_ref, v_ref)


def _mla_project_latent(x, mod, g_pre, wd, g_q, wq, g_kv, wukv, cos, sin):
    bsz, s, d = x.shape
    tm = TM_PROJ
    hd = MLA_HEADS
    out_shape = (
        jax.ShapeDtypeStruct((bsz, hd, s, QK_PAD), BF16),
        jax.ShapeDtypeStruct((bsz, hd, s, QK_PAD), BF16),
        jax.ShapeDtypeStruct((bsz, hd, s, V_PAD), BF16),
    )
    head_spec = lambda w: pl.BlockSpec((None, hd, tm, w), lambda b, i: (b, 0, i, 0))
    return pl.pallas_call(
        _mla_lat_kernel,
        out_shape=out_shape,
        grid=(bsz, s // tm),
        in_specs=[
            pl.BlockSpec((None, tm, d), lambda b, i: (b, i, 0)),
            _mod_spec(0, 0), _mod_spec(0, 1),
            _resident(g_pre.shape), _resident(wd.shape), _resident(g_q.shape), _resident(wq.shape),
            _resident(g_kv.shape), _resident(wukv.shape),
            pl.BlockSpec((tm, LANES), lambda b, i: (i, 0)),
            pl.BlockSpec((tm, LANES), lambda b, i: (i, 0)),
        ],
        out_specs=(head_spec(QK_PAD), head_spec(QK_PAD), head_spec(V_PAD)),
        compiler_params=_cparams(2),
        name="mla_proj_latent",
    )(x, mod, mod, g_pre, wd, g_q, wq, g_kv, wukv, cos, sin)


def _mla_project_context(ctx, mod, g_pre, wd_kv, g_kv, wukv, row):
    bsz, c, d = ctx.shape
    hd = MLA_HEADS
    out_shape = (
        jax.ShapeDtypeStruct((bsz, hd, c, QK_PAD), BF16),
        jax.ShapeDtypeStruct((bsz, hd, c, V_PAD), BF16),
    )
    head_spec = lambda w: pl.BlockSpec((None, hd, c, w), lambda b: (b, 0, 0, 0))
    return pl.pallas_call(
        functools.partial(_mla_ctx_kernel, row=row),
        out_shape=out_shape,
        grid=(bsz,),
        in_specs=[
            pl.BlockSpec((None, c, d), lambda b: (b, 0, 0)),
            _mod_spec(0, 0), _mod_spec(0, 1),
            _resident(g_pre.shape), _resident(wd_kv.shape), _resident(g_kv.shape),
            _resident(wukv.shape),
        ],
        out_specs=(head_spec(QK_PAD), head_spec(V_PAD)),
        compiler_params=_cparams(1),
        name="mla_proj_context",
    )(ctx, mod, mod, g_pre, wd_kv, g_kv, wukv)


def _attn_kernel(q_ref, kc_ref, kl_ref, vc_ref, vl_ref, o_ref):
    q = q_ref[...]
    tq = q.shape[0]

    def scores(k):
        return lax.dot_general(q, k, (((1,), (1,)), ((), ())), preferred_element_type=F32)

    def accumulate(s, v, carry):
        m, acc = carry
        m_new = jnp.maximum(m, jnp.max(s, axis=-1, keepdims=True))
        p = jnp.exp2(s - m_new)
        alpha = jnp.exp2(m - m_new)
        acc = alpha * acc + jnp.dot(p.astype(BF16), v, preferred_element_type=F32)
        return m_new, acc

    n_lat = kl_ref.shape[0] // TK_ATTN
    chunk = lambda j: slice(j * TK_ATTN, (j + 1) * TK_ATTN)
    carry = (jnp.full((tq, 1), -jnp.inf, F32), jnp.zeros((tq, 2 * V_DIM), F32))
    s = scores(kc_ref[...])
    v = vc_ref[...]
    for j in range(n_lat):
        s_next = scores(kl_ref[chunk(j), :])
        carry = accumulate(s, v, carry)
        s, v = s_next, vl_ref[chunk(j), :]
    _, acc = accumulate(s, v, carry)
    o_ref[...] = (acc[:, :V_DIM] / acc[:, V_DIM:]).astype(o_ref.dtype)


def _mla_attention(q, k_ctx, k_lat, v_ctx, v_lat):
    bsz, hd, s, _ = q.shape
    c = k_ctx.shape[2]
    tq = TQ_ATTN
    per_head = lambda rows, w: pl.BlockSpec((None, None, rows, w), lambda b, h, i: (b, h, 0, 0))
    return pl.pallas_call(
        _attn_kernel,
        out_shape=jax.ShapeDtypeStruct((bsz, s, hd * V_DIM), BF16),
        grid=(bsz, hd, s // tq),
        in_specs=[
            pl.BlockSpec((None, None, tq, QK_PAD), lambda b, h, i: (b, h, i, 0)),
            per_head(c, QK_PAD), per_head(s, QK_PAD), per_head(c, V_PAD), per_head(s, V_PAD),
        ],
        out_specs=pl.BlockSpec((None, tq, V_DIM), lambda b, h, i: (b, i, h)),
        compiler_params=_cparams(3),
        name="mla_attn",
    )(q, k_ctx, k_lat, v_ctx, v_lat)


def _residual_and_next_norm(x, y, gate, g_post, g_next, shift, scale, x_out_ref, h_out_ref):
    x_new = x + gate * (_rms(y) * g_post)
    x_out_ref[...] = x_new
    if h_out_ref is not None:
        h_out_ref[...] = _modulated_norm(x_new, g_next, shift, scale).astype(BF16)


def _attn_post_kernel(o_ref, x_ref, wo_ref, gate_ref, sh_ref, sc_ref, gpost_ref, gnext_ref,
                      x_out_ref, h_out_ref):
    b = pl.ds(pl.program_id(0), 1)
    n_blk = 4
    rows = o_ref.shape[0] // n_blk
    blocks = [slice(r * rows, (r + 1) * rows) for r in range(n_blk)]
    ys = [jnp.dot(o_ref[blk, :], wo_ref[...], preferred_element_type=F32) for blk in blocks]
    for blk, y in zip(blocks, ys):
        _residual_and_next_norm(x_ref[blk, :], y, gate_ref[b, :], gpost_ref[...], gnext_ref[...],
                                sh_ref[b, :], sc_ref[b, :], x_out_ref.at[blk, :], h_out_ref.at[blk, :])


def _attn_post(o, x, w_o, mod, layer, g_post, g_next):
    bsz, s, d = x.shape
    tm = TM_POST
    tile = lambda: pl.BlockSpec((None, tm, d), lambda b, i: (b, i, 0))
    return pl.pallas_call(
        _attn_post_kernel,
        out_shape=(jax.ShapeDtypeStruct((bsz, s, d), F32), jax.ShapeDtypeStruct((bsz, s, d), BF16)),
        grid=(bsz, s // tm),
        in_specs=[tile(), tile(), _resident(w_o.shape),
                  _mod_spec(layer, 2), _mod_spec(layer, 3), _mod_spec(layer, 4),
                  _resident(g_post.shape), _resident(g_next.shape)],
        out_specs=(tile(), tile()),
        compiler_params=_cparams(2),
        name="attn_post",
    )(o, x, w_o, mod, mod, mod, g_post, g_next)


def _halo_specs(tm, d, s):
    per = tm // HALO
    last = s // HALO - 1
    main = pl.BlockSpec((None, tm, d), lambda b, i: (b, i, 0))
    prev = pl.BlockSpec((None, HALO, d), lambda b, i: (b, jnp.maximum(i * per - 1, 0), 0))
    nxt = pl.BlockSpec((None, HALO, d), lambda b, i: (b, jnp.minimum((i + 1) * per, last), 0))
    return prev, main, nxt


def _gather_halo(hp_ref, h_ref, hn_ref, hext_ref):
    i = pl.program_id(1)
    tm = h_ref.shape[0]
    hp = hp_ref[...]
    hn = hn_ref[...]
    has_prev = jnp.broadcast_to((i > 0).astype(jnp.int32), hp.shape) > 0
    has_next = jnp.broadcast_to((i < pl.num_programs(1) - 1).astype(jnp.int32), hn.shape) > 0
    hext_ref[0:HALO, :] = jnp.where(has_prev, hp, jnp.zeros_like(hp))
    hext_ref[HALO:HALO + tm, :] = h_ref[...]
    hext_ref[HALO + tm:, :] = jnp.where(has_next, hn, jnp.zeros_like(hn))


def _ffn_kernel(*refs, emit_next):
    (hp_ref, h_ref, hn_ref, x_ref, wup_ref, wdw_ref, bdw_ref, wdn_ref, gate_ref, gpost_ref) = refs[:10]
    if emit_next:
        gnext_ref, sh_ref, sc_ref, x_out_ref, h_out_ref, hext_ref, ua_ref, ub_ref, acc_ref = refs[10:]
    else:
        (x_out_ref, hext_ref, ua_ref, ub_ref, acc_ref), h_out_ref = refs[10:], None
    tm = h_ref.shape[0]
    n_chunks = FFN_DIM // FFN_CHUNK
    assert n_chunks % 2 == 1
    _gather_halo(hp_ref, h_ref, hn_ref, hext_ref)

    def up_project(c, u_ref):
        hext = hext_ref[...]
        cols = lambda blk: pl.ds(pl.multiple_of(blk * FFN_CHUNK, FFN_CHUNK), FFN_CHUNK)
        u_ref[0] = jnp.dot(hext, wup_ref[:, cols(c)], preferred_element_type=F32)
        u_ref[1] = jnp.dot(hext, wup_ref[:, cols(n_chunks + c)], preferred_element_type=F32)

    def conv3(c, u_ref, j):
        w = wdw_ref[j * n_chunks + c]
        return (u_ref[j, pl.ds(HALO - 1, tm), :] * w[0:1]
                + u_ref[j, pl.ds(HALO, tm), :] * w[1:2]
                + u_ref[j, pl.ds(HALO + 1, tm), :] * w[2:3]) + bdw_ref[j * n_chunks + c]

    def down_project(c, u_ref):
        gate = conv3(c, u_ref, 0)
        val = conv3(c, u_ref, 1)
        z = (gate * jax.nn.sigmoid(gate) * val).astype(BF16)
        rows = pl.ds(pl.multiple_of(c * FFN_CHUNK, FFN_CHUNK), FFN_CHUNK)
        acc_ref[...] += jnp.dot(z, wdn_ref[rows, :], preferred_element_type=F32)

    acc_ref[...] = jnp.zeros_like(acc_ref)
    up_project(0, ua_ref)

    def body(i, carry):
        c = 2 * i
        up_project(c + 1, ub_ref)
        down_project(c, ua_ref)
        up_project(c + 2, ua_ref)
        down_project(c + 1, ub_ref)
        return carry

    lax.fori_loop(0, n_chunks // 2, body, 0)
    down_project(n_chunks - 1, ua_ref)
    y = acc_ref[...]

    b = pl.ds(pl.program_id(0), 1)
    if emit_next:
        _residual_and_next_norm(x_ref[...], y, gate_ref[b, :], gpost_ref[...], gnext_ref[...],
                                sh_ref[b, :], sc_ref[b, :], x_out_ref, h_out_ref)
    else:
        _residual_and_next_norm(x_ref[...], y, gate_ref[b, :], gpost_ref[...], None, None, None,
                                x_out_ref, None)


def _conv_ffn(h, x, w_up, w_dw, b_dw, w_down, mod, layer, g_post, g_next=None):
    bsz, s, d = x.shape
    tm = TM_FFN
    emit_next = g_next is not None
    n_blocks = 2 * FFN_DIM // FFN_CHUNK
    w_up = w_up.astype(BF16)
    w_dw = w_dw.reshape(FFN_CONV_WIDTH, n_blocks, FFN_CHUNK).transpose(1, 0, 2)
    b_dw = b_dw.reshape(n_blocks, 1, FFN_CHUNK)
    w_down = w_down.astype(BF16)
    tile = lambda: pl.BlockSpec((None, tm, d), lambda b, i: (b, i, 0))
    in_specs = [*_halo_specs(tm, d, s), tile(), _resident(w_up.shape), _resident(w_dw.shape),
                _resident(b_dw.shape), _resident(w_down.shape), _mod_spec(layer, 5),
                _resident(g_post.shape)]
    args = [h, h, h, x, w_up, w_dw, b_dw, w_down, mod, g_post]
    out_shape = [jax.ShapeDtypeStruct((bsz, s, d), F32)]
    out_specs = [tile()]
    if emit_next:
        in_specs += [_resident(g_next.shape), _mod_spec(layer + 1, 0), _mod_spec(layer + 1, 1)]
        args += [g_next, mod, mod]
        out_shape.append(jax.ShapeDtypeStruct((bsz, s, d), BF16))
        out_specs.append(tile())
    return pl.pallas_call(
        functools.partial(_ffn_kernel, emit_next=emit_next),
        out_shape=tuple(out_shape),
        grid=(bsz, s // tm),
        in_specs=in_specs,
        out_specs=tuple(out_specs),
        scratch_shapes=[pltpu.VMEM((tm + 2 * HALO, d), BF16),
                        pltpu.VMEM((2, tm + 2 * HALO, FFN_CHUNK), F32),
                        pltpu.VMEM((2, tm + 2 * HALO, FFN_CHUNK), F32),
                        pltpu.VMEM((tm, d), F32)],
        compiler_params=_cparams(2),
        name="conv_ffn",
    )(*args)


def _conformer_kernel(hp_ref, h_ref, hn_ref, x_ref, wpw1_ref, bpw1_ref, wdw_ref, bdw_ref, gln_ref,
                      bln_ref, wpw2_ref, bpw2_ref, gate_ref, sh_ref, sc_ref, gpost_ref, gnext_ref,
                      x_out_ref, h_out_ref, hext_ref, glu_ref, conv_ref):
    i = pl.program_id(1)
    tm = h_ref.shape[0]
    rows = tm + 2 * HALO
    d = D_MODEL
    n_blk = d // LANES
    _gather_halo(hp_ref, h_ref, hn_ref, hext_ref)

    first_row = jnp.where(i > 0, 0, HALO)
    end_row = jnp.where(i < pl.num_programs(1) - 1, rows, HALO + tm)
    half = rows // 2
    row_blocks = [slice(0, half), slice(half, rows)]
    us = [jnp.dot(hext_ref[blk, :], wpw1_ref[...], preferred_element_type=F32) for blk in row_blocks]
    for blk, u in zip(row_blocks, us):
        u = u + bpw1_ref[...]
        glu = u[:, :d] * jax.nn.sigmoid(u[:, d:])
        r = lax.broadcasted_iota(jnp.int32, (half, 1), 0) + blk.start
        glu = jnp.where(jnp.logical_and(r >= first_row, r < end_row), glu, 0.0)
        for c in range(n_blk):
            glu_ref[c, blk, :] = glu[:, c * LANES:(c + 1) * LANES]

    pad = (CONV_WIDTH - 1) // 2

    def conv_block(c, carry):
        w = wdw_ref[c]
        for r0 in range(0, tm, CONF_ROWS):
            acc = jnp.broadcast_to(bdw_ref[c], (CONF_ROWS, LANES))
            for k in range(CONV_WIDTH):
                acc = acc + glu_ref[c, pl.ds(HALO - pad + k + r0, CONF_ROWS), :] * w[k:k + 1, :]
            conv_ref[c, r0:r0 + CONF_ROWS, :] = acc
        return carry

    lax.fori_loop(0, n_blk, conv_block, 0)

    n_out = 4
    out_blocks = [slice(r * tm // n_out, (r + 1) * tm // n_out) for r in range(n_out)]
    ys = []
    for blk in out_blocks:
        v = jnp.concatenate([conv_ref[c, blk, :] for c in range(n_blk)], axis=-1)
        mu = jnp.mean(v, axis=-1, keepdims=True)
        var = jnp.mean(jnp.square(v - mu), axis=-1, keepdims=True)
        ln = (v - mu) * lax.rsqrt(var + EPS) * gln_ref[...] + bln_ref[...]
        z = (ln * jax.nn.sigmoid(ln)).astype(BF16)
        ys.append(jnp.dot(z, wpw2_ref[...], preferred_element_type=F32))

    b = pl.ds(pl.program_id(0), 1)
    for blk, y in zip(out_blocks, ys):
        _residual_and_next_norm(x_ref[blk, :], y + bpw2_ref[...], gate_ref[b, :], gpost_ref[...],
                                gnext_ref[...], sh_ref[b, :], sc_ref[b, :],
                                x_out_ref.at[blk, :], h_out_ref.at[blk, :])


def _conformer(h, x, w_pw1, b_pw1, w_dw_blk, b_dw_blk, g_ln, b_ln, w_pw2, b_pw2, mod, layer,
               g_post, g_next):
    bsz, s, d = x.shape
    tm = TM_CONF
    n_blk = d // LANES
    tile = lambda: pl.BlockSpec((None, tm, d), lambda b, i: (b, i, 0))
    consts = [w_pw1, b_pw1, w_dw_blk, b_dw_blk, g_ln, b_ln, w_pw2, b_pw2]
    return pl.pallas_call(
        _conformer_kernel,
        out_shape=(jax.ShapeDtypeStruct((bsz, s, d), F32), jax.ShapeDtypeStruct((bsz, s, d), BF16)),
        grid=(bsz, s // tm),
        in_specs=[*_halo_specs(tm, d, s), tile(), *[_resident(a.shape) for a in consts],
                  _mod_spec(layer, 2), _mod_spec(layer, 3), _mod_spec(layer, 4),
                  _resident(g_post.shape), _resident(g_next.shape)],
        out_specs=(tile(), tile()),
        scratch_shapes=[pltpu.VMEM((tm + 2 * HALO, d), BF16),
                        pltpu.VMEM((n_blk, tm + 2 * HALO, LANES), F32),
                        pltpu.VMEM((n_blk, tm, LANES), F32)],
        compiler_params=_cparams(2),
        name="conformer",
    )(h, h, h, x, *consts, mod, mod, mod, g_post, g_next)


def _rope_tables(seq_len):
    n = ROPE_DIM // 4
    pos = np.arange(seq_len)
    inv = np.float32(ROPE_THETA) ** (-np.arange(n, dtype=np.float32) / np.float32(n))
    ang_r = (pos // GRID_W).astype(np.float32)[:, None] * inv
    ang_c = (pos % GRID_W).astype(np.float32)[:, None] * inv
    zeros = np.zeros((seq_len, LANES - ROPE_DIM), np.float32)
    cos = np.concatenate([np.cos(ang_r), np.cos(ang_r), np.cos(ang_c), np.cos(ang_c), zeros], -1)
    sin = np.concatenate([-np.sin(ang_r), np.sin(ang_r), -np.sin(ang_c), np.sin(ang_c), zeros], -1)
    return jnp.asarray(cos, F32), jnp.asarray(sin, F32)


def _swap_halves(w):
    n = ROPE_DIM // 4
    return jnp.concatenate([w[..., n:2 * n], w[..., :n], w[..., 3 * n:], w[..., 2 * n:3 * n]], -1)


def _pad_lanes(w):
    return jnp.pad(w, [(0, 0)] * (w.ndim - 1) + [(0, LANES - w.shape[-1])])


def _mla_weights(w_dqkv, w_uq):
    kr = w_dqkv[:, Q_LORA + KV_LORA:]
    wd = jnp.concatenate([w_dqkv[:, :Q_LORA + KV_LORA], _pad_lanes(kr), _pad_lanes(_swap_halves(kr))], -1)
    wq = w_uq.reshape(Q_LORA, MLA_HEADS, QK_DIM)
    rope = wq[:, :, NOPE_DIM:]
    wq = jnp.concatenate([wq[:, :, :NOPE_DIM].reshape(Q_LORA, -1),
                          _pad_lanes(rope).reshape(Q_LORA, -1),
                          _pad_lanes(_swap_halves(rope)).reshape(Q_LORA, -1)], -1)
    return wd.astype(BF16), wq.astype(BF16)


def kernel(x, c, ctx, c_ctx, w_mod, b_mod, g_pre_mix, g_post_mix, g_pre_ffn, g_post_ffn, mla_w_dqkv, mla_g_q, mla_w_uq, mla_g_kv, mla_w_ukv, mla_w_o, cv_w_pw1, cv_b_pw1, cv_w_dw, cv_b_dw, cv_g_ln, cv_b_ln, cv_w_pw2, cv_b_pw2, ffn_w_up, ffn_w_dw, ffn_b_dw, ffn_w_down):
    bsz, s, d = x.shape
    assert (bsz, d) == (c.shape[0], D_MODEL) and bsz + 1 <= MOD_ROWS and DEPTH == 2
    row = lambda v: v.reshape(1, -1)

    cc = jnp.concatenate([c, c_ctx[None, :], jnp.zeros((MOD_ROWS - bsz - 1, d), F32)], axis=0)
    mod = _adaln_mod(cc, w_mod, b_mod)

    cos, sin = _rope_tables(s)
    wd, wq = _mla_weights(mla_w_dqkv[0], mla_w_uq[0])
    wukv = mla_w_ukv[0].astype(BF16)
    g0 = row(g_pre_mix[0])
    g_kv = row(mla_g_kv[0])
    q, k_lat, v_lat = _mla_project_latent(x, mod, g0, wd, row(mla_g_q[0]), wq, g_kv, wukv, cos, sin)
    k_ctx, v_ctx = _mla_project_context(ctx, mod, g0, wd[:, Q_LORA:Q_LORA + KV_LORA + LANES], g_kv,
                                        wukv, bsz)
    o = _mla_attention(q, k_ctx, k_lat, v_ctx, v_lat)
    x1, h = _attn_post(o, x, mla_w_o[0].astype(BF16), mod, 0, row(g_post_mix[0]), row(g_pre_ffn[0]))
    x2, h = _conv_ffn(h, x1, ffn_w_up[0], ffn_w_dw[0], ffn_b_dw[0], ffn_w_down[0], mod, 0,
                      row(g_post_ffn[0]), row(g_pre_mix[1]))

    n_blk = d // LANES
    w_dw_blk = jnp.pad(cv_w_dw[0], ((0, 1), (0, 0))).reshape(CONV_WIDTH + 1, n_blk, LANES)
    w_dw_blk = w_dw_blk.transpose(1, 0, 2)
    b_dw_blk = cv_b_dw[0].reshape(n_blk, 1, LANES)
    x3, h = _conformer(h, x2, cv_w_pw1[0].astype(BF16), row(cv_b_pw1[0]), w_dw_blk, b_dw_blk,
                       row(cv_g_ln[0]), row(cv_b_ln[0]), cv_w_pw2[0].astype(BF16), row(cv_b_pw2[0]),
                       mod, 1, row(g_post_mix[1]), row(g_pre_ffn[1]))
    (x4,) = _conv_ffn(h, x3, ffn_w_up[1], ffn_w_dw[1], ffn_b_dw[1], ffn_w_down[1], mod, 1,
                      row(g_post_ffn[1]))
    return x4
```

```python
import functools
import math

import jax
import jax.numpy as jnp
import numpy as np
from jax import lax
from jax.experimental import pallas as pl
from jax.experimental.pallas import tpu as pltpu

F32 = jnp.float32
BF16 = jnp.bfloat16

D_MODEL = 1024
DEPTH = 2
GRID_W = 64
MLA_HEADS = 8
Q_LORA = 384
KV_LORA = 128
NOPE_DIM = 128
ROPE_DIM = 64
V_DIM = 128
QK_DIM = NOPE_DIM + ROPE_DIM
ROPE_THETA = 10000.0
CONV_WIDTH = 31
FFN_DIM = 2816
FFN_CONV_WIDTH = 3
EPS = 1e-6

LANES = 128
QK_PAD = 2 * LANES
V_PAD = 2 * V_DIM
MOD_ROWS = 8
HALO = 16
VMEM_LIMIT = 56 * 1024 * 1024

TM_PROJ = 512
TQ_ATTN = 1024
TK_ATTN = 2048
TM_POST = 512
TM_FFN = 512
FFN_CHUNK = 256
TM_CONF = 512
CONF_ROWS = 128


def _rms(x):
    return x * lax.rsqrt(jnp.mean(x * x, axis=-1, keepdims=True) + EPS)


def _modulated_norm(x, g, shift, scale):
    return (_rms(x) * g) * (1.0 + scale) + shift


def _cparams(n_axes):
    return pltpu.CompilerParams(
        dimension_semantics=("parallel",) * n_axes, vmem_limit_bytes=VMEM_LIMIT)


def _resident(shape):
    zeros = (0,) * len(shape)
    return pl.BlockSpec(shape, lambda *_: zeros, pipeline_mode=pl.Buffered(1))


def _mod_kernel(cc_ref, w_ref, b_ref, o_ref):
    a = cc_ref[...]
    a = a * jax.nn.sigmoid(a)
    o_ref[...] = jnp.dot(a.astype(BF16), w_ref[...].astype(BF16),
                         preferred_element_type=F32) + b_ref[...]


def _adaln_mod(cc, w_mod, b_mod):
    d = D_MODEL
    return pl.pallas_call(
        _mod_kernel,
        out_shape=jax.ShapeDtypeStruct((DEPTH, 6, MOD_ROWS, d), F32),
        grid=(DEPTH, 6),
        in_specs=[
            pl.BlockSpec((MOD_ROWS, d), lambda i, j: (0, 0)),
            pl.BlockSpec((None, d, d), lambda i, j: (i, 0, j)),
            pl.BlockSpec((None, 1, d), lambda i, j: (i, 0, j)),
        ],
        out_specs=pl.BlockSpec((None, None, MOD_ROWS, d), lambda i, j: (i, j, 0, 0)),
        compiler_params=_cparams(2),
        name="adaln_mod",
    )(cc, w_mod, b_mod.reshape(DEPTH, 1, 6 * d))


def _mod_spec(layer, chunk):
    return pl.BlockSpec((None, None, MOD_ROWS, D_MODEL), lambda *_: (layer, chunk, 0, 0))


def _kv_project(h, wd_kv, g_kv, w_ukv):
    dkv = jnp.dot(h, wd_kv, preferred_element_type=F32)
    nkv = (_rms(dkv[:, :KV_LORA]) * g_kv).astype(BF16)
    kv = jnp.dot(nkv, w_ukv, preferred_element_type=F32)
    return kv, dkv[:, KV_LORA:]


def _store_kv(kv, kr, k_ref, v_ref):
    for hh in range(MLA_HEADS):
        base = hh * (NOPE_DIM + V_DIM)
        k_ref[hh, :, 0:NOPE_DIM] = kv[:, base:base + NOPE_DIM].astype(BF16)
        k_ref[hh, :, NOPE_DIM:QK_PAD] = kr
        v_ref[hh, :, 0:V_DIM] = kv[:, base + NOPE_DIM:base + NOPE_DIM + V_DIM].astype(BF16)
        v_ref[hh, :, V_DIM:] = jnp.ones((kv.shape[0], V_DIM), BF16)


def _mla_lat_kernel(x_ref, sh_ref, sc_ref, g_ref, wd_ref, gq_ref, wq_ref, gkv_ref, wukv_ref,
                    cos_ref, sin_ref, q_ref, k_ref, v_ref):
    b = pl.program_id(0)
    shift = sh_ref[pl.ds(b, 1), :]
    scale = sc_ref[pl.ds(b, 1), :]
    h = _modulated_norm(x_ref[...], g_ref[...], shift, scale).astype(BF16)
    cos = cos_ref[...]
    sin = sin_ref[...]

    kv, kr_raw = _kv_project(h, wd_ref[:, Q_LORA:], gkv_ref[...], wukv_ref[...])
    kr = (kr_raw[:, :LANES] * cos + kr_raw[:, LANES:] * sin).astype(BF16)
    _store_kv(kv, kr, k_ref, v_ref)

    cq = jnp.dot(h, wd_ref[:, :Q_LORA], preferred_element_type=F32)
    nq = (_rms(cq) * (gq_ref[...] * (math.log2(math.e) / math.sqrt(QK_DIM)))).astype(BF16)
    n_nope = MLA_HEADS * NOPE_DIM
    n_rope = MLA_HEADS * LANES
    q_nope = jnp.dot(nq, wq_ref[:, :n_nope], preferred_element_type=F32)
    q_rope = jnp.dot(nq, wq_ref[:, n_nope:n_nope + n_rope], preferred_element_type=F32)
    q_swap = jnp.dot(nq, wq_ref[:, n_nope + n_rope:], preferred_element_type=F32)
    for hh in range(MLA_HEADS):
        q_ref[hh, :, 0:NOPE_DIM] = q_nope[:, hh * NOPE_DIM:(hh + 1) * NOPE_DIM].astype(BF16)
        cols = slice(hh * LANES, (hh + 1) * LANES)
        q_ref[hh, :, NOPE_DIM:QK_PAD] = (q_rope[:, cols] * cos + q_swap[:, cols] * sin).astype(BF16)


def _mla_ctx_kernel(x_ref, sh_ref, sc_ref, g_ref, wd_ref, gkv_ref, wukv_ref, k_ref, v_ref, *, row):
    shift = sh_ref[row:row + 1, :]
    scale = sc_ref[row:row + 1, :]
    h = _modulated_norm(x_ref[...], g_ref[...], shift, scale).astype(BF16)
    kv, kr_raw = _kv_project(h, wd_ref[...], gkv_ref[...], wukv_ref[...])
    _store_kv(kv, kr_raw[:, :LANES].astype(BF16), k_ref, v_ref)


def _mla_project_latent(x, mod, g_pre, wd, g_q, wq, g_kv, wukv, cos, sin):
    bsz, s, d = x.shape
    tm = TM_PROJ
    hd = MLA_HEADS
    out_shape = (
        jax.ShapeDtypeStruct((bsz, hd, s, QK_PAD), BF16),
        jax.ShapeDtypeStruct((bsz, hd, s, QK_PAD), BF16),
        jax.ShapeDtypeStruct((bsz, hd, s, V_PAD), BF16),
    )
    head_spec = lambda w: pl.BlockSpec((None, hd, tm, w), lambda b, i: (b, 0, i, 0))
    return pl.pallas_call(
        _mla_lat_kernel,
        out_shape=out_shape,
        grid=(bsz, s // tm),
        in_specs=[
            pl.BlockSpec((None, tm, d), lambda b, i: (b, i, 0)),
            _mod_spec(0, 0), _mod_spec(0, 1),
            _resident(g_pre.shape), _resident(wd.shape), _resident(g_q.shape), _resident(wq.shape),
            _resident(g_kv.shape), _resident(wukv.shape),
            pl.BlockSpec((tm, LANES), lambda b, i: (i, 0)),
            pl.BlockSpec((tm, LANES), lambda b, i: (i, 0)),
        ],
        out_specs=(head_spec(QK_PAD), head_spec(QK_PAD), head_spec(V_PAD)),
        compiler_params=_cparams(2),
        name="mla_proj_latent",
    )(x, mod, mod, g_pre, wd, g_q, wq, g_kv, wukv, cos, sin)


def _mla_project_context(ctx, mod, g_pre, wd_kv, g_kv, wukv, row):
    bsz, c, d = ctx.shape
    hd = MLA_HEADS
    out_shape = (
        jax.ShapeDtypeStruct((bsz, hd, c, QK_PAD), BF16),
        jax.ShapeDtypeStruct((bsz, hd, c, V_PAD), BF16),
    )
    head_spec = lambda w: pl.BlockSpec((None, hd, c, w), lambda b: (b, 0, 0, 0))
    return pl.pallas_call(
        functools.partial(_mla_ctx_kernel, row=row),
        out_shape=out_shape,
        grid=(bsz,),
        in_specs=[
            pl.BlockSpec((None, c, d), lambda b: (b, 0, 0)),
            _mod_spec(0, 0), _mod_spec(0, 1),
            _resident(g_pre.shape), _resident(wd_kv.shape), _resident(g_kv.shape),
            _resident(wukv.shape),
        ],
        out_specs=(head_spec(QK_PAD), head_spec(V_PAD)),
        compiler_params=_cparams(1),
        name="mla_proj_context",
    )(ctx, mod, mod, g_pre, wd_kv, g_kv, wukv)


def _attn_kernel(q_ref, kc_ref, kl_ref, vc_ref, vl_ref, o_ref):
    q = q_ref[...]
    tq = q.shape[0]

    def scores(k):
        return lax.dot_general(q, k, (((1,), (1,)), ((), ())), preferred_element_type=F32)

    def accumulate(s, v, carry):
        m, acc = carry
        m_new = jnp.maximum(m, jnp.max(s, axis=-1, keepdims=True))
        p = jnp.exp2(s - m_new)
        alpha = jnp.exp2(m - m_new)
        acc = alpha * acc + jnp.dot(p.astype(BF16), v, preferred_element_type=F32)
        return m_new, acc

    n_lat = kl_ref.shape[0] // TK_ATTN
    chunk = lambda j: slice(j * TK_ATTN, (j + 1) * TK_ATTN)
    carry = (jnp.full((tq, 1), -jnp.inf, F32), jnp.zeros((tq, 2 * V_DIM), F32))
    s = scores(kc_ref[...])
    v = vc_ref[...]
    for j in range(n_lat):
        s_next = scores(kl_ref[chunk(j), :])
        carry = accumulate(s, v, carry)
        s, v = s_next, vl_ref[chunk(j), :]
    _, acc = accumulate(s, v, carry)
    o_ref[...] = (acc[:, :V_DIM] / acc[:, V_DIM:]).astype(o_ref.dtype)


def _mla_attention(q, k_ctx, k_lat, v_ctx, v_lat):
    bsz, hd, s, _ = q.shape
    c = k_ctx.shape[2]
    tq = TQ_ATTN
    per_head = lambda rows, w: pl.BlockSpec((None, None, rows, w), lambda b, h, i: (b, h, 0, 0))
    return pl.pallas_call(
        _attn_kernel,
        out_shape=jax.ShapeDtypeStruct((bsz, s, hd * V_DIM), BF16),
        grid=(bsz, hd, s // tq),
        in_specs=[
            pl.BlockSpec((None, None, tq, QK_PAD), lambda b, h, i: (b, h, i, 0)),
            per_head(c, QK_PAD), per_head(s, QK_PAD), per_head(c, V_PAD), per_head(s, V_PAD),
        ],
        out_specs=pl.BlockSpec((None, tq, V_DIM), lambda b, h, i: (b, i, h)),
        compiler_params=_cparams(3),
        name="mla_attn",
    )(q, k_ctx, k_lat, v_ctx, v_lat)


def _residual_and_next_norm(x, y, gate, g_post, g_next, shift, scale, x_out_ref, h_out_ref):
    x_new = x + gate * (_rms(y) * g_post)
    x_out_ref[...] = x_new
    if h_out_ref is not None:
        h_out_ref[...] = _modulated_norm(x_new, g_next, shift, scale).astype(BF16)


def _attn_post_kernel(o_ref, x_ref, wo_ref, gate_ref, sh_ref, sc_ref, gpost_ref, gnext_ref,
                      x_out_ref, h_out_ref):
    b = pl.ds(pl.program_id(0), 1)
    n_blk = 4
    rows = o_ref.shape[0] // n_blk
    blocks = [slice(r * rows, (r + 1) * rows) for r in range(n_blk)]
    ys = [jnp.dot(o_ref[blk, :], wo_ref[...], preferred_element_type=F32) for blk in blocks]
    for blk, y in zip(blocks, ys):
        _residual_and_next_norm(x_ref[blk, :], y, gate_ref[b, :], gpost_ref[...], gnext_ref[...],
                                sh_ref[b, :], sc_ref[b, :], x_out_ref.at[blk, :], h_out_ref.at[blk, :])


def _attn_post(o, x, w_o, mod, layer, g_post, g_next):
    bsz, s, d = x.shape
    tm = TM_POST
    tile = lambda: pl.BlockSpec((None, tm, d), lambda b, i: (b, i, 0))
    return pl.pallas_call(
        _attn_post_kernel,
        out_shape=(jax.ShapeDtypeStruct((bsz, s, d), F32), jax.ShapeDtypeStruct((bsz, s, d), BF16)),
        grid=(bsz, s // tm),
        in_specs=[tile(), tile(), _resident(w_o.shape),
                  _mod_spec(layer, 2), _mod_spec(layer, 3), _mod_spec(layer, 4),
                  _resident(g_post.shape), _resident(g_next.shape)],
        out_specs=(tile(), tile()),
        compiler_params=_cparams(2),
        name="attn_post",
    )(o, x, w_o, mod, mod, mod, g_post, g_next)


def _halo_specs(tm, d, s):
    per = tm // HALO
    last = s // HALO - 1
    main = pl.BlockSpec((None, tm, d), lambda b, i: (b, i, 0))
    prev = pl.BlockSpec((None, HALO, d), lambda b, i: (b, jnp.maximum(i * per - 1, 0), 0))
    nxt = pl.BlockSpec((None, HALO, d), lambda b, i: (b, jnp.minimum((i + 1) * per, last), 0))
    return prev, main, nxt


def _gather_halo(hp_ref, h_ref, hn_ref, hext_ref):
    i = pl.program_id(1)
    tm = h_ref.shape[0]
    hp = hp_ref[...]
    hn = hn_ref[...]
    has_prev = jnp.broadcast_to((i > 0).astype(jnp.int32), hp.shape) > 0
    has_next = jnp.broadcast_to((i < pl.num_programs(1) - 1).astype(jnp.int32), hn.shape) > 0
    hext_ref[0:HALO, :] = jnp.where(has_prev, hp, jnp.zeros_like(hp))
    hext_ref[HALO:HALO + tm, :] = h_ref[...]
    hext_ref[HALO + tm:, :] = jnp.where(has_next, hn, jnp.zeros_like(hn))


def _ffn_kernel(*refs, emit_next):
    (hp_ref, h_ref, hn_ref, x_ref, wup_ref, wdw_ref, bdw_ref, wdn_ref, gate_ref, gpost_ref) = refs[:10]
    if emit_next:
        gnext_ref, sh_ref, sc_ref, x_out_ref, h_out_ref, hext_ref, ua_ref, ub_ref, uc_ref = refs[10:]
    else:
        (x_out_ref, hext_ref, ua_ref, ub_ref, uc_ref), h_out_ref = refs[10:], None
    tm = h_ref.shape[0]
    n_chunks = FFN_DIM // FFN_CHUNK
    assert n_chunks % 2 == 1
    _gather_halo(hp_ref, h_ref, hn_ref, hext_ref)

    def up_project(c, u_ref):
        hext = hext_ref[...]
        cols = lambda blk: pl.ds(blk * FFN_CHUNK, FFN_CHUNK)
        u_ref[0] = jnp.dot(hext, wup_ref[:, cols(c)], preferred_element_type=F32)
        u_ref[1] = jnp.dot(hext, wup_ref[:, cols(n_chunks + c)], preferred_element_type=F32)

    def conv3(c, u_ref, j):
        w = wdw_ref[j * n_chunks + c]
        return (u_ref[j, pl.ds(HALO - 1, tm), :] * w[0:1]
                + u_ref[j, pl.ds(HALO, tm), :] * w[1:2]
                + u_ref[j, pl.ds(HALO + 1, tm), :] * w[2:3]) + bdw_ref[j * n_chunks + c]

    def down_project(c, u_ref):
        gate = conv3(c, u_ref, 0)
        val = conv3(c, u_ref, 1)
        z = (gate * jax.nn.sigmoid(gate) * val).astype(BF16)
        rows = pl.ds(c * FFN_CHUNK, FFN_CHUNK)
        return jnp.dot(z, wdn_ref[rows, :], preferred_element_type=F32)

    bufs = (ua_ref, ub_ref, uc_ref)
    up_project(0, bufs[0])
    up_project(1, bufs[1])
    y = None
    for c in range(n_chunks):
        if c + 2 < n_chunks:
            up_project(c + 2, bufs[(c + 2) % 3])
        part = down_project(c, bufs[c % 3])
        y = part if y is None else y + part

    b = pl.ds(pl.program_id(0), 1)
    if emit_next:
        _residual_and_next_norm(x_ref[...], y, gate_ref[b, :], gpost_ref[...], gnext_ref[...],
                                sh_ref[b, :], sc_ref[b, :], x_out_ref, h_out_ref)
    else:
        _residual_and_next_norm(x_ref[...], y, gate_ref[b, :], gpost_ref[...], None, None, None,
                                x_out_ref, None)


def _conv_ffn(h, x, w_up, w_dw, b_dw, w_down, mod, layer, g_post, g_next=None):
    bsz, s, d = x.shape
    tm = TM_FFN
    emit_next = g_next is not None
    n_blocks = 2 * FFN_DIM // FFN_CHUNK
    w_up = w_up.astype(BF16)
    w_dw = w_dw.reshape(FFN_CONV_WIDTH, n_blocks, FFN_CHUNK).transpose(1, 0, 2)
    b_dw = b_dw.reshape(n_blocks, 1, FFN_CHUNK)
    w_down = w_down.astype(BF16)
    tile = lambda: pl.BlockSpec((None, tm, d), lambda b, i: (b, i, 0))
    in_specs = [*_halo_specs(tm, d, s), tile(), _resident(w_up.shape), _resident(w_dw.shape),
                _resident(b_dw.shape), _resident(w_down.shape), _mod_spec(layer, 5),
                _resident(g_post.shape)]
    args = [h, h, h, x, w_up, w_dw, b_dw, w_down, mod, g_post]
    out_shape = [jax.ShapeDtypeStruct((bsz, s, d), F32)]
    out_specs = [tile()]
    if emit_next:
        in_specs += [_resident(g_next.shape), _mod_spec(layer + 1, 0), _mod_spec(layer + 1, 1)]
        args += [g_next, mod, mod]
        out_shape.append(jax.ShapeDtypeStruct((bsz, s, d), BF16))
        out_specs.append(tile())
    return pl.pallas_call(
        functools.partial(_ffn_kernel, emit_next=emit_next),
        out_shape=tuple(out_shape),
        grid=(bsz, s // tm),
        in_specs=in_specs,
        out_specs=tuple(out_specs),
        scratch_shapes=[pltpu.VMEM((tm + 2 * HALO, d), BF16),
                        pltpu.VMEM((2, tm + 2 * HALO, FFN_CHUNK), F32),
                        pltpu.VMEM((2, tm + 2 * HALO, FFN_CHUNK), F32),
                        pltpu.VMEM((2, tm + 2 * HALO, FFN_CHUNK), F32)],
        compiler_params=_cparams(2),
        name="conv_ffn",
    )(*args)


def _conformer_kernel(hp_ref, h_ref, hn_ref, x_ref, wpw1_ref, bpw1_ref, wdw_ref, bdw_ref, gln_ref,
                      bln_ref, wpw2_ref, bpw2_ref, gate_ref, sh_ref, sc_ref, gpost_ref, gnext_ref,
                      x_out_ref, h_out_ref, hext_ref, glu_ref, conv_ref):
    i = pl.program_id(1)
    tm = h_ref.shape[0]
    rows = tm + 2 * HALO
    d = D_MODEL
    n_blk = d // LANES
    _gather_halo(hp_ref, h_ref, hn_ref, hext_ref)

    first_row = jnp.where(i > 0, 0, HALO)
    end_row = jnp.where(i < pl.num_programs(1) - 1, rows, HALO + tm)
    half = rows // 2
    row_blocks = [slice(0, half), slice(half, rows)]
    us = [jnp.dot(hext_ref[blk, :], wpw1_ref[...], preferred_element_type=F32) for blk in row_blocks]
    for blk, u in zip(row_blocks, us):
        u = u + bpw1_ref[...]
        glu = u[:, :d] * jax.nn.sigmoid(u[:, d:])
        r = lax.broadcasted_iota(jnp.int32, (half, 1), 0) + blk.start
        glu = jnp.where(jnp.logical_and(r >= first_row, r < end_row), glu, 0.0)
        for c in range(n_blk):
            glu_ref[c, blk, :] = glu[:, c * LANES:(c + 1) * LANES]

    pad = (CONV_WIDTH - 1) // 2

    def conv_block(c, carry):
        w = wdw_ref[c]
        for r0 in range(0, tm, CONF_ROWS):
            acc = jnp.broadcast_to(bdw_ref[c], (CONF_ROWS, LANES))
            for k in range(CONV_WIDTH):
                acc = acc + glu_ref[c, pl.ds(HALO - pad + k + r0, CONF_ROWS), :] * w[k:k + 1, :]
            conv_ref[c, r0:r0 + CONF_ROWS, :] = acc
        return carry

    lax.fori_loop(0, n_blk, conv_block, 0)

    n_out = 4
    out_blocks = [slice(r * tm // n_out, (r + 1) * tm // n_out) for r in range(n_out)]
    ys = []
    for blk in out_blocks:
        v = jnp.concatenate([conv_ref[c, blk, :] for c in range(n_blk)], axis=-1)
        mu = jnp.mean(v, axis=-1, keepdims=True)
        var = jnp.mean(jnp.square(v - mu), axis=-1, keepdims=True)
        ln = (v - mu) * lax.rsqrt(var + EPS) * gln_ref[...] + bln_ref[...]
        z = (ln * jax.nn.sigmoid(ln)).astype(BF16)
        ys.append(jnp.dot(z, wpw2_ref[...], preferred_element_type=F32))

    b = pl.ds(pl.program_id(0), 1)
    for blk, y in zip(out_blocks, ys):
        _residual_and_next_norm(x_ref[blk, :], y + bpw2_ref[...], gate_ref[b, :], gpost_ref[...],
                                gnext_ref[...], sh_ref[b, :], sc_ref[b, :],
                                x_out_ref.at[blk, :], h_out_ref.at[blk, :])


def _conformer(h, x, w_pw1, b_pw1, w_dw_blk, b_dw_blk, g_ln, b_ln, w_pw2, b_pw2, mod, layer,
               g_post, g_next):
    bsz, s, d = x.shape
    tm = TM_CONF
    n_blk = d // LANES
    tile = lambda: pl.BlockSpec((None, tm, d), lambda b, i: (b, i, 0))
    consts = [w_pw1, b_pw1, w_dw_blk, b_dw_blk, g_ln, b_ln, w_pw2, b_pw2]
    return pl.pallas_call(
        _conformer_kernel,
        out_shape=(jax.ShapeDtypeStruct((bsz, s, d), F32), jax.ShapeDtypeStruct((bsz, s, d), BF16)),
        grid=(bsz, s // tm),
        in_specs=[*_halo_specs(tm, d, s), tile(), *[_resident(a.shape) for a in consts],
                  _mod_spec(layer, 2), _mod_spec(layer, 3), _mod_spec(layer, 4),
                  _resident(g_post.shape), _resident(g_next.shape)],
        out_specs=(tile(), tile()),
        scratch_shapes=[pltpu.VMEM((tm + 2 * HALO, d), BF16),
                        pltpu.VMEM((n_blk, tm + 2 * HALO, LANES), F32),
                        pltpu.VMEM((n_blk, tm, LANES), F32)],
        compiler_params=_cparams(2),
        name="conformer",
    )(h, h, h, x, *consts, mod, mod, mod, g_post, g_next)


def _rope_tables(seq_len):
    n = ROPE_DIM // 4
    pos = np.arange(seq_len)
    inv = np.float32(ROPE_THETA) ** (-np.arange(n, dtype=np.float32) / np.float32(n))
    ang_r = (pos // GRID_W).astype(np.float32)[:, None] * inv
    ang_c = (pos % GRID_W).astype(np.float32)[:, None] * inv
    zeros = np.zeros((seq_len, LANES - ROPE_DIM), np.float32)
    cos = np.concatenate([np.cos(ang_r), np.cos(ang_r), np.cos(ang_c), np.cos(ang_c), zeros], -1)
    sin = np.concatenate([-np.sin(ang_r), np.sin(ang_r), -np.sin(ang_c), np.sin(ang_c), zeros], -1)
    return jnp.asarray(cos, F32), jnp.asarray(sin, F32)


def _swap_halves(w):
    n = ROPE_DIM // 4
    return jnp.concatenate([w[..., n:2 * n], w[..., :n], w[..., 3 * n:], w[..., 2 * n:3 * n]], -1)


def _pad_lanes(w):
    return jnp.pad(w, [(0, 0)] * (w.ndim - 1) + [(0, LANES - w.shape[-1])])


def _mla_weights(w_dqkv, w_uq):
    kr = w_dqkv[:, Q_LORA + KV_LORA:]
    wd = jnp.concatenate([w_dqkv[:, :Q_LORA + KV_LORA], _pad_lanes(kr), _pad_lanes(_swap_halves(kr))], -1)
    wq = w_uq.reshape(Q_LORA, MLA_HEADS, QK_DIM)
    rope = wq[:, :, NOPE_DIM:]
    wq = jnp.concatenate([wq[:, :, :NOPE_DIM].reshape(Q_LORA, -1),
                          _pad_lanes(rope).reshape(Q_LORA, -1),
                          _pad_lanes(_swap_halves(rope)).reshape(Q_LORA, -1)], -1)
    return wd.astype(BF16), wq.astype(BF16)


def kernel(x, c, ctx, c_ctx, w_mod, b_mod, g_pre_mix, g_post_mix, g_pre_ffn, g_post_ffn, mla_w_dqkv, mla_g_q, mla_w_uq, mla_g_kv, mla_w_ukv, mla_w_o, cv_w_pw1, cv_b_pw1, cv_w_dw, cv_b_dw, cv_g_ln, cv_b_ln, cv_w_pw2, cv_b_pw2, ffn_w_up, ffn_w_dw, ffn_b_dw, ffn_w_down):
    bsz, s, d = x.shape
    assert (bsz, d) == (c.shape[0], D_MODEL) and bsz + 1 <= MOD_ROWS and DEPTH == 2
    row = lambda v: v.reshape(1, -1)

    cc = jnp.concatenate([c, c_ctx[None, :], jnp.zeros((MOD_ROWS - bsz - 1, d), F32)], axis=0)
    mod = _adaln_mod(cc, w_mod, b_mod)

    cos, sin = _rope_tables(s)
    wd, wq = _mla_weights(mla_w_dqkv[0], mla_w_uq[0])
    wukv = mla_w_ukv[0].astype(BF16)
    g0 = row(g_pre_mix[0])
    g_kv = row(mla_g_kv[0])
    q, k_lat, v_lat = _mla_project_latent(x, mod, g0, wd, row(mla_g_q[0]), wq, g_kv, wukv, cos, sin)
    k_ctx, v_ctx = _mla_project_context(ctx, mod, g0, wd[:, Q_LORA:Q_LORA + KV_LORA + LANES], g_kv,
                                        wukv, bsz)
    o = _mla_attention(q, k_ctx, k_lat, v_ctx, v_lat)
    x1, h = _attn_post(o, x, mla_w_o[0].astype(BF16), mod, 0, row(g_post_mix[0]), row(g_pre_ffn[0]))
    x2, h = _conv_ffn(h, x1, ffn_w_up[0], ffn_w_dw[0], ffn_b_dw[0], ffn_w_down[0], mod, 0,
                      row(g_post_ffn[0]), row(g_pre_mix[1]))

    n_blk = d // LANES
    w_dw_blk = jnp.pad(cv_w_dw[0], ((0, 1), (0, 0))).reshape(CONV_WIDTH + 1, n_blk, LANES)
    w_dw_blk = w_dw_blk.transpose(1, 0, 2)
    b_dw_blk = cv_b_dw[0].reshape(n_blk, 1, LANES)
    x3, h = _conformer(h, x2, cv_w_pw1[0].astype(BF16), row(cv_b_pw1[0]), w_dw_blk, b_dw_blk,
                       row(cv_g_ln[0]), row(cv_b_ln[0]), cv_w_pw2[0].astype(BF16), row(cv_b_pw2[0]),
                       mod, 1, row(g_post_mix[1]), row(g_pre_ffn[1]))
    (x4,) = _conv_ffn(h, x3, ffn_w_up[1], ffn_w_dw[1], ffn_b_dw[1], ffn_w_down[1], mod, 1,
                      row(g_post_ffn[1]))
    return x4
```

```python
import functools
import math

import jax
import jax.numpy as jnp
import numpy as np
from jax import lax
from jax.experimental import pallas as pl
from jax.experimental.pallas import tpu as pltpu

F32 = jnp.float32
BF16 = jnp.bfloat16

D_MODEL = 1024
DEPTH = 2
GRID_W = 64
MLA_HEADS = 8
Q_LORA = 384
KV_LORA = 128
NOPE_DIM = 128
ROPE_DIM = 64
V_DIM = 128
QK_DIM = NOPE_DIM + ROPE_DIM
ROPE_THETA = 10000.0
CONV_WIDTH = 31
FFN_DIM = 2816
FFN_CONV_WIDTH = 3
EPS = 1e-6

LANES = 128
QK_PAD = 2 * LANES
V_PAD = 2 * V_DIM
MOD_ROWS = 8
HALO = 16
VMEM_LIMIT = 56 * 1024 * 1024

TM_PROJ = 512
TQ_ATTN = 1024
TK_ATTN = 2048
TM_POST = 512
TM_FFN = 512
FFN_CHUNK = 256
FFN_BUFFERS = 6
TM_CONF = 512
CONF_ROWS = 128


def _rms(x):
    return x * lax.rsqrt(jnp.mean(x * x, axis=-1, keepdims=True) + EPS)


def _modulated_norm(x, g, shift, scale):
    return (_rms(x) * g) * (1.0 + scale) + shift


def _cparams(n_axes):
    return pltpu.CompilerParams(
        dimension_semantics=("parallel",) * n_axes, vmem_limit_bytes=VMEM_LIMIT)


def _resident(shape):
    zeros = (0,) * len(shape)
    return pl.BlockSpec(shape, lambda *_: zeros, pipeline_mode=pl.Buffered(1))


def _mod_kernel(cc_ref, w_ref, b_ref, o_ref):
    a = cc_ref[...]
    a = a * jax.nn.sigmoid(a)
    o_ref[...] = jnp.dot(a.astype(BF16), w_ref[...].astype(BF16),
                         preferred_element_type=F32) + b_ref[...]


def _adaln_mod(cc, w_mod, b_mod):
    d = D_MODEL
    return pl.pallas_call(
        _mod_kernel,
        out_shape=jax.ShapeDtypeStruct((DEPTH, 6, MOD_ROWS, d), F32),
        grid=(DEPTH, 6),
        in_specs=[
            pl.BlockSpec((MOD_ROWS, d), lambda i, j: (0, 0)),
            pl.BlockSpec((None, d, d), lambda i, j: (i, 0, j)),
            pl.BlockSpec((None, 1, d), lambda i, j: (i, 0, j)),
        ],
        out_specs=pl.BlockSpec((None, None, MOD_ROWS, d), lambda i, j: (i, j, 0, 0)),
        compiler_params=_cparams(2),
        name="adaln_mod",
    )(cc, w_mod, b_mod.reshape(DEPTH, 1, 6 * d))


def _mod_spec(layer, chunk):
    return pl.BlockSpec((None, None, MOD_ROWS, D_MODEL), lambda *_: (layer, chunk, 0, 0))


def _kv_project(h, wd_kv, g_kv, w_ukv):
    dkv = jnp.dot(h, wd_kv, preferred_element_type=F32)
    nkv = (_rms(dkv[:, :KV_LORA]) * g_kv).astype(BF16)
    kv = jnp.dot(nkv, w_ukv, preferred_element_type=F32)
    return kv, dkv[:, KV_LORA:]


def _store_kv(kv, kr, k_ref, v_ref):
    for hh in range(MLA_HEADS):
        base = hh * (NOPE_DIM + V_DIM)
        k_ref[hh, :, 0:NOPE_DIM] = kv[:, base:base + NOPE_DIM].astype(BF16)
        k_ref[hh, :, NOPE_DIM:QK_PAD] = kr
        v_ref[hh, :, 0:V_DIM] = kv[:, base + NOPE_DIM:base + NOPE_DIM + V_DIM].astype(BF16)
        v_ref[hh, :, V_DIM:] = jnp.ones((kv.shape[0], V_DIM), BF16)


def _mla_lat_kernel(x_ref, sh_ref, sc_ref, g_ref, wd_ref, gq_ref, wq_ref, gkv_ref, wukv_ref,
                    cos_ref, sin_ref, q_ref, k_ref, v_ref):
    b = pl.program_id(0)
    shift = sh_ref[pl.ds(b, 1), :]
    scale = sc_ref[pl.ds(b, 1), :]
    h = _modulated_norm(x_ref[...], g_ref[...], shift, scale).astype(BF16)
    cos = cos_ref[...]
    sin = sin_ref[...]

    kv, kr_raw = _kv_project(h, wd_ref[:, Q_LORA:], gkv_ref[...], wukv_ref[...])
    kr = (kr_raw[:, :LANES] * cos + kr_raw[:, LANES:] * sin).astype(BF16)
    _store_kv(kv, kr, k_ref, v_ref)

    cq = jnp.dot(h, wd_ref[:, :Q_LORA], preferred_element_type=F32)
    nq = (_rms(cq) * (gq_ref[...] * (math.log2(math.e) / math.sqrt(QK_DIM)))).astype(BF16)
    n_nope = MLA_HEADS * NOPE_DIM
    n_rope = MLA_HEADS * LANES
    q_nope = jnp.dot(nq, wq_ref[:, :n_nope], preferred_element_type=F32)
    q_rope = jnp.dot(nq, wq_ref[:, n_nope:n_nope + n_rope], preferred_element_type=F32)
    q_swap = jnp.dot(nq, wq_ref[:, n_nope + n_rope:], preferred_element_type=F32)
    for hh in range(MLA_HEADS):
        q_ref[hh, :, 0:NOPE_DIM] = q_nope[:, hh * NOPE_DIM:(hh + 1) * NOPE_DIM].astype(BF16)
        cols = slice(hh * LANES, (hh + 1) * LANES)
        q_ref[hh, :, NOPE_DIM:QK_PAD] = (q_rope[:, cols] * cos + q_swap[:, cols] * sin).astype(BF16)


def _mla_ctx_kernel(x_ref, sh_ref, sc_ref, g_ref, wd_ref, gkv_ref, wukv_ref, k_ref, v_ref, *, row):
    shift = sh_ref[row:row + 1, :]
    scale = sc_ref[row:row + 1, :]
    h = _modulated_norm(x_ref[...], g_ref[...], shift, scale).astype(BF16)
    kv, kr_raw = _kv_project(h, wd_ref[...], gkv_ref[...], wukv_ref[...])
    _store_kv(kv, kr_raw[:, :LANES].astype(BF16), k_ref, v_ref)


def _mla_project_latent(x, mod, g_pre, wd, g_q, wq, g_kv, wukv, cos, sin):
    bsz, s, d = x.shape
    tm = TM_PROJ
    hd = MLA_HEADS
    out_shape = (
        jax.ShapeDtypeStruct((bsz, hd, s, QK_PAD), BF16),
        jax.ShapeDtypeStruct((bsz, hd, s, QK_PAD), BF16),
        jax.ShapeDtypeStruct((bsz, hd, s, V_PAD), BF16),
    )
    head_spec = lambda w: pl.BlockSpec((None, hd, tm, w), lambda b, i: (b, 0, i, 0))
    return pl.pallas_call(
        _mla_lat_kernel,
        out_shape=out_shape,
        grid=(bsz, s // tm),
        in_specs=[
            pl.BlockSpec((None, tm, d), lambda b, i: (b, i, 0)),
            _mod_spec(0, 0), _mod_spec(0, 1),
            _resident(g_pre.shape), _resident(wd.shape), _resident(g_q.shape), _resident(wq.shape),
            _resident(g_kv.shape), _resident(wukv.shape),
            pl.BlockSpec((tm, LANES), lambda b, i: (i, 0)),
            pl.BlockSpec((tm, LANES), lambda b, i: (i, 0)),
        ],
        out_specs=(head_spec(QK_PAD), head_spec(QK_PAD), head_spec(V_PAD)),
        compiler_params=_cparams(2),
        name="mla_proj_latent",
    )(x, mod, mod, g_pre, wd, g_q, wq, g_kv, wukv, cos, sin)


def _mla_project_context(ctx, mod, g_pre, wd_kv, g_kv, wukv, row):
    bsz, c, d = ctx.shape
    hd = MLA_HEADS
    out_shape = (
        jax.ShapeDtypeStruct((bsz, hd, c, QK_PAD), BF16),
        jax.ShapeDtypeStruct((bsz, hd, c, V_PAD), BF16),
    )
    head_spec = lambda w: pl.BlockSpec((None, hd, c, w), lambda b: (b, 0, 0, 0))
    return pl.pallas_call(
        functools.partial(_mla_ctx_kernel, row=row),
        out_shape=out_shape,
        grid=(bsz,),
        in_specs=[
            pl.BlockSpec((None, c, d), lambda b: (b, 0, 0)),
            _mod_spec(0, 0), _mod_spec(0, 1),
            _resident(g_pre.shape), _resident(wd_kv.shape), _resident(g_kv.shape),
            _resident(wukv.shape),
        ],
        out_specs=(head_spec(QK_PAD), head_spec(V_PAD)),
        compiler_params=_cparams(1),
        name="mla_proj_context",
    )(ctx, mod, mod, g_pre, wd_kv, g_kv, wukv)


def _attn_kernel(q_ref, kc_ref, kl_ref, vc_ref, vl_ref, o_ref):
    q = q_ref[...]
    tq = q.shape[0]

    def scores(k):
        return lax.dot_general(q, k, (((1,), (1,)), ((), ())), preferred_element_type=F32)

    def accumulate(s, v, carry):
        m, acc = carry
        m_new = jnp.maximum(m, jnp.max(s, axis=-1, keepdims=True))
        p = jnp.exp2(s - m_new)
        alpha = jnp.exp2(m - m_new)
        acc = alpha * acc + jnp.dot(p.astype(BF16), v, preferred_element_type=F32)
        return m_new, acc

    n_lat = kl_ref.shape[0] // TK_ATTN
    chunk = lambda j: slice(j * TK_ATTN, (j + 1) * TK_ATTN)
    carry = (jnp.full((tq, 1), -jnp.inf, F32), jnp.zeros((tq, 2 * V_DIM), F32))
    s = scores(kc_ref[...])
    v = vc_ref[...]
    for j in range(n_lat):
        s_next = scores(kl_ref[chunk(j), :])
        carry = accumulate(s, v, carry)
        s, v = s_next, vl_ref[chunk(j), :]
    _, acc = accumulate(s, v, carry)
    o_ref[...] = (acc[:, :V_DIM] / acc[:, V_DIM:]).astype(o_ref.dtype)


def _mla_attention(q, k_ctx, k_lat, v_ctx, v_lat):
    bsz, hd, s, _ = q.shape
    c = k_ctx.shape[2]
    tq = TQ_ATTN
    per_head = lambda rows, w: pl.BlockSpec((None, None, rows, w), lambda b, h, i: (b, h, 0, 0))
    return pl.pallas_call(
        _attn_kernel,
        out_shape=jax.ShapeDtypeStruct((bsz, s, hd * V_DIM), BF16),
        grid=(bsz, hd, s // tq),
        in_specs=[
            pl.BlockSpec((None, None, tq, QK_PAD), lambda b, h, i: (b, h, i, 0)),
            per_head(c, QK_PAD), per_head(s, QK_PAD), per_head(c, V_PAD), per_head(s, V_PAD),
        ],
        out_specs=pl.BlockSpec((None, tq, V_DIM), lambda b, h, i: (b, i, h)),
        compiler_params=_cparams(3),
        name="mla_attn",
    )(q, k_ctx, k_lat, v_ctx, v_lat)


def _residual_and_next_norm(x, y, gate, g_post, g_next, shift, scale, x_out_ref, h_out_ref):
    x_new = x + gate * (_rms(y) * g_post)
    x_out_ref[...] = x_new
    if h_out_ref is not None:
        h_out_ref[...] = _modulated_norm(x_new, g_next, shift, scale).astype(BF16)


def _attn_post_kernel(o_ref, x_ref, wo_ref, gate_ref, sh_ref, sc_ref, gpost_ref, gnext_ref,
                      x_out_ref, h_out_ref):
    b = pl.ds(pl.program_id(0), 1)
    n_blk = 4
    rows = o_ref.shape[0] // n_blk
    blocks = [slice(r * rows, (r + 1) * rows) for r in range(n_blk)]
    ys = [jnp.dot(o_ref[blk, :], wo_ref[...], preferred_element_type=F32) for blk in blocks]
    for blk, y in zip(blocks, ys):
        _residual_and_next_norm(x_ref[blk, :], y, gate_ref[b, :], gpost_ref[...], gnext_ref[...],
                                sh_ref[b, :], sc_ref[b, :], x_out_ref.at[blk, :], h_out_ref.at[blk, :])


def _attn_post(o, x, w_o, mod, layer, g_post, g_next):
    bsz, s, d = x.shape
    tm = TM_POST
    tile = lambda: pl.BlockSpec((None, tm, d), lambda b, i: (b, i, 0))
    return pl.pallas_call(
        _attn_post_kernel,
        out_shape=(jax.ShapeDtypeStruct((bsz, s, d), F32), jax.ShapeDtypeStruct((bsz, s, d), BF16)),
        grid=(bsz, s // tm),
        in_specs=[tile(), tile(), _resident(w_o.shape),
                  _mod_spec(layer, 2), _mod_spec(layer, 3), _mod_spec(layer, 4),
                  _resident(g_post.shape), _resident(g_next.shape)],
        out_specs=(tile(), tile()),
        compiler_params=_cparams(2),
        name="attn_post",
    )(o, x, w_o, mod, mod, mod, g_post, g_next)


def _halo_specs(tm, d, s):
    per = tm // HALO
    last = s // HALO - 1
    main = pl.BlockSpec((None, tm, d), lambda b, i: (b, i, 0))
    prev = pl.BlockSpec((None, HALO, d), lambda b, i: (b, jnp.maximum(i * per - 1, 0), 0))
    nxt = pl.BlockSpec((None, HALO, d), lambda b, i: (b, jnp.minimum((i + 1) * per, last), 0))
    return prev, main, nxt


def _gather_halo(hp_ref, h_ref, hn_ref, hext_ref):
    i = pl.program_id(1)
    tm = h_ref.shape[0]
    hp = hp_ref[...]
    hn = hn_ref[...]
    has_prev = jnp.broadcast_to((i > 0).astype(jnp.int32), hp.shape) > 0
    has_next = jnp.broadcast_to((i < pl.num_programs(1) - 1).astype(jnp.int32), hn.shape) > 0
    hext_ref[0:HALO, :] = jnp.where(has_prev, hp, jnp.zeros_like(hp))
    hext_ref[HALO:HALO + tm, :] = h_ref[...]
    hext_ref[HALO + tm:, :] = jnp.where(has_next, hn, jnp.zeros_like(hn))


def _ffn_kernel(*refs, emit_next):
    (hp_ref, h_ref, hn_ref, x_ref, wup_ref, wdw_ref, bdw_ref, wdn_ref, gate_ref, gpost_ref) = refs[:10]
    if emit_next:
        gnext_ref, sh_ref, sc_ref, x_out_ref, h_out_ref, hext_ref, ubuf_ref = refs[10:]
    else:
        (x_out_ref, hext_ref, ubuf_ref), h_out_ref = refs[10:], None
    tm = h_ref.shape[0]
    n_chunks = FFN_DIM // FFN_CHUNK
    n_buf = ubuf_ref.shape[0]
    _gather_halo(hp_ref, h_ref, hn_ref, hext_ref)

    def up_project(c, u_ref):
        hext = hext_ref[...]
        cols = lambda blk: pl.ds(blk * FFN_CHUNK, FFN_CHUNK)
        u_ref[0] = jnp.dot(hext, wup_ref[:, cols(c)], preferred_element_type=F32)
        u_ref[1] = jnp.dot(hext, wup_ref[:, cols(n_chunks + c)], preferred_element_type=F32)

    def conv3(c, u_ref, j):
        w = wdw_ref[j * n_chunks + c]
        return (u_ref[j, pl.ds(HALO - 1, tm), :] * w[0:1]
                + u_ref[j, pl.ds(HALO, tm), :] * w[1:2]
                + u_ref[j, pl.ds(HALO + 1, tm), :] * w[2:3]) + bdw_ref[j * n_chunks + c]

    def down_project(c, u_ref):
        gate = conv3(c, u_ref, 0)
        val = conv3(c, u_ref, 1)
        z = (gate * jax.nn.sigmoid(gate) * val).astype(BF16)
        rows = pl.ds(c * FFN_CHUNK, FFN_CHUNK)
        return jnp.dot(z, wdn_ref[rows, :], preferred_element_type=F32)

    ahead = n_buf - 1
    for c in range(ahead):
        up_project(c, ubuf_ref.at[c])
    y = None
    for c in range(n_chunks):
        if c + ahead < n_chunks:
            up_project(c + ahead, ubuf_ref.at[(c + ahead) % n_buf])
        part = down_project(c, ubuf_ref.at[c % n_buf])
        y = part if y is None else y + part

    b = pl.ds(pl.program_id(0), 1)
    if emit_next:
        _residual_and_next_norm(x_ref[...], y, gate_ref[b, :], gpost_ref[...], gnext_ref[...],
                                sh_ref[b, :], sc_ref[b, :], x_out_ref, h_out_ref)
    else:
        _residual_and_next_norm(x_ref[...], y, gate_ref[b, :], gpost_ref[...], None, None, None,
                                x_out_ref, None)


def _conv_ffn(h, x, w_up, w_dw, b_dw, w_down, mod, layer, g_post, g_next=None):
    bsz, s, d = x.shape
    tm = TM_FFN
    emit_next = g_next is not None
    n_blocks = 2 * FFN_DIM // FFN_CHUNK
    w_up = w_up.astype(BF16)
    w_dw = w_dw.reshape(FFN_CONV_WIDTH, n_blocks, FFN_CHUNK).transpose(1, 0, 2)
    b_dw = b_dw.reshape(n_blocks, 1, FFN_CHUNK)
    w_down = w_down.astype(BF16)
    tile = lambda: pl.BlockSpec((None, tm, d), lambda b, i: (b, i, 0))
    in_specs = [*_halo_specs(tm, d, s), tile(), _resident(w_up.shape), _resident(w_dw.shape),
                _resident(b_dw.shape), _resident(w_down.shape), _mod_spec(layer, 5),
                _resident(g_post.shape)]
    args = [h, h, h, x, w_up, w_dw, b_dw, w_down, mod, g_post]
    out_shape = [jax.ShapeDtypeStruct((bsz, s, d), F32)]
    out_specs = [tile()]
    if emit_next:
        in_specs += [_resident(g_next.shape), _mod_spec(layer + 1, 0), _mod_spec(layer + 1, 1)]
        args += [g_next, mod, mod]
        out_shape.append(jax.ShapeDtypeStruct((bsz, s, d), BF16))
        out_specs.append(tile())
    return pl.pallas_call(
        functools.partial(_ffn_kernel, emit_next=emit_next),
        out_shape=tuple(out_shape),
        grid=(bsz, s // tm),
        in_specs=in_specs,
        out_specs=tuple(out_specs),
        scratch_shapes=[pltpu.VMEM((tm + 2 * HALO, d), BF16),
                        pltpu.VMEM((FFN_BUFFERS, 2, tm + 2 * HALO, FFN_CHUNK), F32)],
        compiler_params=_cparams(2),
        name="conv_ffn",
    )(*args)


def _conformer_kernel(hp_ref, h_ref, hn_ref, x_ref, wpw1_ref, bpw1_ref, wdw_ref, bdw_ref, gln_ref,
                      bln_ref, wpw2_ref, bpw2_ref, gate_ref, sh_ref, sc_ref, gpost_ref, gnext_ref,
                      x_out_ref, h_out_ref, hext_ref, glu_ref, conv_ref):
    i = pl.program_id(1)
    tm = h_ref.shape[0]
    rows = tm + 2 * HALO
    d = D_MODEL
    n_blk = d // LANES
    _gather_halo(hp_ref, h_ref, hn_ref, hext_ref)

    first_row = jnp.where(i > 0, 0, HALO)
    end_row = jnp.where(i < pl.num_programs(1) - 1, rows, HALO + tm)
    half = rows // 2
    row_blocks = [slice(0, half), slice(half, rows)]
    us = [jnp.dot(hext_ref[blk, :], wpw1_ref[...], preferred_element_type=F32) for blk in row_blocks]
    for blk, u in zip(row_blocks, us):
        u = u + bpw1_ref[...]
        glu = u[:, :d] * jax.nn.sigmoid(u[:, d:])
        r = lax.broadcasted_iota(jnp.int32, (half, 1), 0) + blk.start
        glu = jnp.where(jnp.logical_and(r >= first_row, r < end_row), glu, 0.0)
        for c in range(n_blk):
            glu_ref[c, blk, :] = glu[:, c * LANES:(c + 1) * LANES]

    pad = (CONV_WIDTH - 1) // 2

    def conv_block(c, carry):
        w = wdw_ref[c]
        for r0 in range(0, tm, CONF_ROWS):
            acc = jnp.broadcast_to(bdw_ref[c], (CONF_ROWS, LANES))
            for k in range(CONV_WIDTH):
                acc = acc + glu_ref[c, pl.ds(HALO - pad + k + r0, CONF_ROWS), :] * w[k:k + 1, :]
            conv_ref[c, r0:r0 + CONF_ROWS, :] = acc
        return carry

    lax.fori_loop(0, n_blk, conv_block, 0)

    n_out = 4
    out_blocks = [slice(r * tm // n_out, (r + 1) * tm // n_out) for r in range(n_out)]
    ys = []
    for blk in out_blocks:
        v = jnp.concatenate([conv_ref[c, blk, :] for c in range(n_blk)], axis=-1)
        mu = jnp.mean(v, axis=-1, keepdims=True)
        var = jnp.mean(jnp.square(v - mu), axis=-1, keepdims=True)
        ln = (v - mu) * lax.rsqrt(var + EPS) * gln_ref[...] + bln_ref[...]
        z = (ln * jax.nn.sigmoid(ln)).astype(BF16)
        ys.append(jnp.dot(z, wpw2_ref[...], preferred_element_type=F32))

    b = pl.ds(pl.program_id(0), 1)
    for blk, y in zip(out_blocks, ys):
        _residual_and_next_norm(x_ref[blk, :], y + bpw2_ref[...], gate_ref[b, :], gpost_ref[...],
                                gnext_ref[...], sh_ref[b, :], sc_ref[b, :],
                                x_out_ref.at[blk, :], h_out_ref.at[blk, :])


def _conformer(h, x, w_pw1, b_pw1, w_dw_blk, b_dw_blk, g_ln, b_ln, w_pw2, b_pw2, mod, layer,
               g_post, g_next):
    bsz, s, d = x.shape
    tm = TM_CONF
    n_blk = d // LANES
    tile = lambda: pl.BlockSpec((None, tm, d), lambda b, i: (b, i, 0))
    consts = [w_pw1, b_pw1, w_dw_blk, b_dw_blk, g_ln, b_ln, w_pw2, b_pw2]
    return pl.pallas_call(
        _conformer_kernel,
        out_shape=(jax.ShapeDtypeStruct((bsz, s, d), F32), jax.ShapeDtypeStruct((bsz, s, d), BF16)),
        grid=(bsz, s // tm),
        in_specs=[*_halo_specs(tm, d, s), tile(), *[_resident(a.shape) for a in consts],
                  _mod_spec(layer, 2), _mod_spec(layer, 3), _mod_spec(layer, 4),
                  _resident(g_post.shape), _resident(g_next.shape)],
        out_specs=(tile(), tile()),
        scratch_shapes=[pltpu.VMEM((tm + 2 * HALO, d), BF16),
                        pltpu.VMEM((n_blk, tm + 2 * HALO, LANES), F32),
                        pltpu.VMEM((n_blk, tm, LANES), F32)],
        compiler_params=_cparams(2),
        name="conformer",
    )(h, h, h, x, *consts, mod, mod, mod, g_post, g_next)


def _rope_tables(seq_len):
    n = ROPE_DIM // 4
    pos = np.arange(seq_len)
    inv = np.float32(ROPE_THETA) ** (-np.arange(n, dtype=np.float32) / np.float32(n))
    ang_r = (pos // GRID_W).astype(np.float32)[:, None] * inv
    ang_c = (pos % GRID_W).astype(np.float32)[:, None] * inv
    zeros = np.zeros((seq_len, LANES - ROPE_DIM), np.float32)
    cos = np.concatenate([np.cos(ang_r), np.cos(ang_r), np.cos(ang_c), np.cos(ang_c), zeros], -1)
    sin = np.concatenate([-np.sin(ang_r), np.sin(ang_r), -np.sin(ang_c), np.sin(ang_c), zeros], -1)
    return jnp.asarray(cos, F32), jnp.asarray(sin, F32)


def _swap_halves(w):
    n = ROPE_DIM // 4
    return jnp.concatenate([w[..., n:2 * n], w[..., :n], w[..., 3 * n:], w[..., 2 * n:3 * n]], -1)


def _pad_lanes(w):
    return jnp.pad(w, [(0, 0)] * (w.ndim - 1) + [(0, LANES - w.shape[-1])])


def _mla_weights(w_dqkv, w_uq):
    kr = w_dqkv[:, Q_LORA + KV_LORA:]
    wd = jnp.concatenate([w_dqkv[:, :Q_LORA + KV_LORA], _pad_lanes(kr), _pad_lanes(_swap_halves(kr))], -1)
    wq = w_uq.reshape(Q_LORA, MLA_HEADS, QK_DIM)
    rope = wq[:, :, NOPE_DIM:]
    wq = jnp.concatenate([wq[:, :, :NOPE_DIM].reshape(Q_LORA, -1),
                          _pad_lanes(rope).reshape(Q_LORA, -1),
                          _pad_lanes(_swap_halves(rope)).reshape(Q_LORA, -1)], -1)
    return wd.astype(BF16), wq.astype(BF16)


def kernel(x, c, ctx, c_ctx, w_mod, b_mod, g_pre_mix, g_post_mix, g_pre_ffn, g_post_ffn, mla_w_dqkv, mla_g_q, mla_w_uq, mla_g_kv, mla_w_ukv, mla_w_o, cv_w_pw1, cv_b_pw1, cv_w_dw, cv_b_dw, cv_g_ln, cv_b_ln, cv_w_pw2, cv_b_pw2, ffn_w_up, ffn_w_dw, ffn_b_dw, ffn_w_down):
    bsz, s, d = x.shape
    assert (bsz, d) == (c.shape[0], D_MODEL) and bsz + 1 <= MOD_ROWS and DEPTH == 2
    row = lambda v: v.reshape(1, -1)

    cc = jnp.concatenate([c, c_ctx[None, :], jnp.zeros((MOD_ROWS - bsz - 1, d), F32)], axis=0)
    mod = _adaln_mod(cc, w_mod, b_mod)

    cos, sin = _rope_tables(s)
    wd, wq = _mla_weights(mla_w_dqkv[0], mla_w_uq[0])
    wukv = mla_w_ukv[0].astype(BF16)
    g0 = row(g_pre_mix[0])
    g_kv = row(mla_g_kv[0])
    q, k_lat, v_lat = _mla_project_latent(x, mod, g0, wd, row(mla_g_q[0]), wq, g_kv, wukv, cos, sin)
    k_ctx, v_ctx = _mla_project_context(ctx, mod, g0, wd[:, Q_LORA:Q_LORA + KV_LORA + LANES], g_kv,
                                        wukv, bsz)
    o = _mla_attention(q, k_ctx, k_lat, v_ctx, v_lat)
    x1, h = _attn_post(o, x, mla_w_o[0].astype(BF16), mod, 0, row(g_post_mix[0]), row(g_pre_ffn[0]))
    x2, h = _conv_ffn(h, x1, ffn_w_up[0], ffn_w_dw[0], ffn_b_dw[0], ffn_w_down[0], mod, 0,
                      row(g_post_ffn[0]), row(g_pre_mix[1]))

    n_blk = d // LANES
    w_dw_blk = jnp.pad(cv_w_dw[0], ((0, 1), (0, 0))).reshape(CONV_WIDTH + 1, n_blk, LANES)
    w_dw_blk = w_dw_blk.transpose(1, 0, 2)
    b_dw_blk = cv_b_dw[0].reshape(n_blk, 1, LANES)
    x3, h = _conformer(h, x2, cv_w_pw1[0].astype(BF16), row(cv_b_pw1[0]), w_dw_blk, b_dw_blk,
                       row(cv_g_ln[0]), row(cv_b_ln[0]), cv_w_pw2[0].astype(BF16), row(cv_b_pw2[0]),
                       mod, 1, row(g_post_mix[1]), row(g_pre_ffn[1]))
    (x4,) = _conv_ffn(h, x3, ffn_w_up[1], ffn_w_dw[1], ffn_b_dw[1], ffn_w_down[1], mod, 1,
                      row(g_post_ffn[1]))
    return x4
```

```python
import functools
import math

import jax
import jax.numpy as jnp
import numpy as np
from jax import lax
from jax.experimental import pallas as pl
from jax.experimental.pallas import tpu as pltpu

F32 = jnp.float32
BF16 = jnp.bfloat16

D_MODEL = 1024
DEPTH = 2
GRID_W = 64
MLA_HEADS = 8
Q_LORA = 384
KV_LORA = 128
NOPE_DIM = 128
ROPE_DIM = 64
V_DIM = 128
QK_DIM = NOPE_DIM + ROPE_DIM
ROPE_THETA = 10000.0
CONV_WIDTH = 31
FFN_DIM = 2816
FFN_CONV_WIDTH = 3
EPS = 1e-6

LANES = 128
QK_PAD = 2 * LANES
V_PAD = 2 * V_DIM
MOD_ROWS = 8
HALO = 16
VMEM_LIMIT = 56 * 1024 * 1024

TM_PROJ = 512
TQ_ATTN = 1024
TK_ATTN = 2048
TM_POST = 512
TM_FFN = 512
FFN_CHUNK = 256
FFN_BUFFERS = 6
TM_CONF = 512
CONF_ROWS = 128


def _rms(x):
    return x * lax.rsqrt(jnp.mean(x * x, axis=-1, keepdims=True) + EPS)


def _modulated_norm(x, g, shift, scale):
    return (_rms(x) * g) * (1.0 + scale) + shift


def _cparams(n_axes):
    return pltpu.CompilerParams(
        dimension_semantics=("parallel",) * n_axes, vmem_limit_bytes=VMEM_LIMIT)


def _resident(shape):
    zeros = (0,) * len(shape)
    return pl.BlockSpec(shape, lambda *_: zeros, pipeline_mode=pl.Buffered(1))


def _mod_kernel(cc_ref, w_ref, b_ref, o_ref):
    a = cc_ref[...]
    a = a * jax.nn.sigmoid(a)
    o_ref[...] = jnp.dot(a.astype(BF16), w_ref[...].astype(BF16),
                         preferred_element_type=F32) + b_ref[...]


def _adaln_mod(cc, w_mod, b_mod):
    d = D_MODEL
    return pl.pallas_call(
        _mod_kernel,
        out_shape=jax.ShapeDtypeStruct((DEPTH, 6, MOD_ROWS, d), F32),
        grid=(DEPTH, 6),
        in_specs=[
            pl.BlockSpec((MOD_ROWS, d), lambda i, j: (0, 0)),
            pl.BlockSpec((None, d, d), lambda i, j: (i, 0, j)),
            pl.BlockSpec((None, 1, d), lambda i, j: (i, 0, j)),
        ],
        out_specs=pl.BlockSpec((None, None, MOD_ROWS, d), lambda i, j: (i, j, 0, 0)),
        compiler_params=_cparams(2),
        name="adaln_mod",
    )(cc, w_mod, b_mod.reshape(DEPTH, 1, 6 * d))


def _mod_spec(layer, chunk):
    return pl.BlockSpec((None, None, MOD_ROWS, D_MODEL), lambda *_: (layer, chunk, 0, 0))


def _kv_project(h, wd_kv, g_kv, w_ukv):
    dkv = jnp.dot(h, wd_kv, preferred_element_type=F32)
    nkv = (_rms(dkv[:, :KV_LORA]) * g_kv).astype(BF16)
    kv = jnp.dot(nkv, w_ukv, preferred_element_type=F32)
    return kv, dkv[:, KV_LORA:]


def _store_kv(kv, kr, k_ref, v_ref):
    for hh in range(MLA_HEADS):
        base = hh * (NOPE_DIM + V_DIM)
        k_ref[hh, :, 0:NOPE_DIM] = kv[:, base:base + NOPE_DIM].astype(BF16)
        k_ref[hh, :, NOPE_DIM:QK_PAD] = kr
        v_ref[hh, :, 0:V_DIM] = kv[:, base + NOPE_DIM:base + NOPE_DIM + V_DIM].astype(BF16)
        v_ref[hh, :, V_DIM:] = jnp.ones((kv.shape[0], V_DIM), BF16)


def _mla_lat_kernel(x_ref, sh_ref, sc_ref, g_ref, wd_ref, gq_ref, wq_ref, gkv_ref, wukv_ref,
                    cos_ref, sin_ref, q_ref, k_ref, v_ref):
    b = pl.program_id(0)
    shift = sh_ref[pl.ds(b, 1), :]
    scale = sc_ref[pl.ds(b, 1), :]
    h = _modulated_norm(x_ref[...], g_ref[...], shift, scale).astype(BF16)
    cos = cos_ref[...]
    sin = sin_ref[...]

    kv, kr_raw = _kv_project(h, wd_ref[:, Q_LORA:], gkv_ref[...], wukv_ref[...])
    kr = (kr_raw[:, :LANES] * cos + kr_raw[:, LANES:] * sin).astype(BF16)
    _store_kv(kv, kr, k_ref, v_ref)

    cq = jnp.dot(h, wd_ref[:, :Q_LORA], preferred_element_type=F32)
    nq = (_rms(cq) * (gq_ref[...] * (math.log2(math.e) / math.sqrt(QK_DIM)))).astype(BF16)
    n_nope = MLA_HEADS * NOPE_DIM
    n_rope = MLA_HEADS * LANES
    q_nope = jnp.dot(nq, wq_ref[:, :n_nope], preferred_element_type=F32)
    q_rope = jnp.dot(nq, wq_ref[:, n_nope:n_nope + n_rope], preferred_element_type=F32)
    q_swap = jnp.dot(nq, wq_ref[:, n_nope + n_rope:], preferred_element_type=F32)
    for hh in range(MLA_HEADS):
        q_ref[hh, :, 0:NOPE_DIM] = q_nope[:, hh * NOPE_DIM:(hh + 1) * NOPE_DIM].astype(BF16)
        cols = slice(hh * LANES, (hh + 1) * LANES)
        q_ref[hh, :, NOPE_DIM:QK_PAD] = (q_rope[:, cols] * cos + q_swap[:, cols] * sin).astype(BF16)


def _mla_ctx_kernel(x_ref, sh_ref, sc_ref, g_ref, wd_ref, gkv_ref, wukv_ref, k_ref, v_ref, *, row):
    shift = sh_ref[row:row + 1, :]
    scale = sc_ref[row:row + 1, :]
    h = _modulated_norm(x_ref[...], g_ref[...], shift, scale).astype(BF16)
    kv, kr_raw = _kv_project(h, wd_ref[...], gkv_ref[...], wukv_ref[...])
    _store_kv(kv, kr_raw[:, :LANES].astype(BF16), k_ref, v_ref)


def _mla_project_latent(x, mod, g_pre, wd, g_q, wq, g_kv, wukv, cos, sin):
    bsz, s, d = x.shape
    tm = TM_PROJ
    hd = MLA_HEADS
    out_shape = (
        jax.ShapeDtypeStruct((bsz, hd, s, QK_PAD), BF16),
        jax.ShapeDtypeStruct((bsz, hd, s, QK_PAD), BF16),
        jax.ShapeDtypeStruct((bsz, hd, s, V_PAD), BF16),
    )
    head_spec = lambda w: pl.BlockSpec((None, hd, tm, w), lambda b, i: (b, 0, i, 0))
    return pl.pallas_call(
        _mla_lat_kernel,
        out_shape=out_shape,
        grid=(bsz, s // tm),
        in_specs=[
            pl.BlockSpec((None, tm, d), lambda b, i: (b, i, 0)),
            _mod_spec(0, 0), _mod_spec(0, 1),
            _resident(g_pre.shape), _resident(wd.shape), _resident(g_q.shape), _resident(wq.shape),
            _resident(g_kv.shape), _resident(wukv.shape),
            pl.BlockSpec((tm, LANES), lambda b, i: (i, 0)),
            pl.BlockSpec((tm, LANES), lambda b, i: (i, 0)),
        ],
        out_specs=(head_spec(QK_PAD), head_spec(QK_PAD), head_spec(V_PAD)),
        compiler_params=_cparams(2),
        name="mla_proj_latent",
    )(x, mod, mod, g_pre, wd, g_q, wq, g_kv, wukv, cos, sin)


def _mla_project_context(ctx, mod, g_pre, wd_kv, g_kv, wukv, row):
    bsz, c, d = ctx.shape
    hd = MLA_HEADS
    out_shape = (
        jax.ShapeDtypeStruct((bsz, hd, c, QK_PAD), BF16),
        jax.ShapeDtypeStruct((bsz, hd, c, V_PAD), BF16),
    )
    head_spec = lambda w: pl.BlockSpec((None, hd, c, w), lambda b: (b, 0, 0, 0))
    return pl.pallas_call(
        functools.partial(_mla_ctx_kernel, row=row),
        out_shape=out_shape,
        grid=(bsz,),
        in_specs=[
            pl.BlockSpec((None, c, d), lambda b: (b, 0, 0)),
            _mod_spec(0, 0), _mod_spec(0, 1),
            _resident(g_pre.shape), _resident(wd_kv.shape), _resident(g_kv.shape),
            _resident(wukv.shape),
        ],
        out_specs=(head_spec(QK_PAD), head_spec(V_PAD)),
        compiler_params=_cparams(1),
        name="mla_proj_context",
    )(ctx, mod, mod, g_pre, wd_kv, g_kv, wukv)


def _attn_kernel(q_ref, kc_ref, kl_ref, vc_ref, vl_ref, o_ref):
    q = q_ref[...]
    tq = q.shape[0]

    def scores(k):
        return lax.dot_general(q, k, (((1,), (1,)), ((), ())), preferred_element_type=F32)

    def accumulate(s, v, carry):
        m, acc = carry
        m_new = jnp.maximum(m, jnp.max(s, axis=-1, keepdims=True))
        p = jnp.exp2(s - m_new)
        alpha = jnp.exp2(m - m_new)
        acc = alpha * acc + jnp.dot(p.astype(BF16), v, preferred_element_type=F32)
        return m_new, acc

    n_lat = kl_ref.shape[0] // TK_ATTN
    chunk = lambda j: slice(j * TK_ATTN, (j + 1) * TK_ATTN)
    carry = (jnp.full((tq, 1), -jnp.inf, F32), jnp.zeros((tq, 2 * V_DIM), F32))
    s = scores(kc_ref[...])
    v = vc_ref[...]
    for j in range(n_lat):
        s_next = scores(kl_ref[chunk(j), :])
        carry = accumulate(s, v, carry)
        s, v = s_next, vl_ref[chunk(j), :]
    _, acc = accumulate(s, v, carry)
    o_ref[...] = (acc[:, :V_DIM] / acc[:, V_DIM:]).astype(o_ref.dtype)


def _mla_attention(q, k_ctx, k_lat, v_ctx, v_lat):
    bsz, hd, s, _ = q.shape
    c = k_ctx.shape[2]
    tq = TQ_ATTN
    per_head = lambda rows, w: pl.BlockSpec((None, None, rows, w), lambda b, h, i: (b, h, 0, 0))
    return pl.pallas_call(
        _attn_kernel,
        out_shape=jax.ShapeDtypeStruct((bsz, s, hd * V_DIM), BF16),
        grid=(bsz, hd, s // tq),
        in_specs=[
            pl.BlockSpec((None, None, tq, QK_PAD), lambda b, h, i: (b, h, i, 0)),
            per_head(c, QK_PAD), per_head(s, QK_PAD), per_head(c, V_PAD), per_head(s, V_PAD),
        ],
        out_specs=pl.BlockSpec((None, tq, V_DIM), lambda b, h, i: (b, i, h)),
        compiler_params=_cparams(3),
        name="mla_attn",
    )(q, k_ctx, k_lat, v_ctx, v_lat)


def _residual_and_next_norm(x, y, gate, g_post, g_next, shift, scale, x_out_ref, h_out_ref):
    x_new = x + gate * (_rms(y) * g_post)
    x_out_ref[...] = x_new
    if h_out_ref is not None:
        h_out_ref[...] = _modulated_norm(x_new, g_next, shift, scale).astype(BF16)


def _attn_post_kernel(o_ref, x_ref, wo_ref, gate_ref, sh_ref, sc_ref, gpost_ref, gnext_ref,
                      x_out_ref, h_out_ref):
    b = pl.ds(pl.program_id(0), 1)
    n_blk = 4
    rows = o_ref.shape[0] // n_blk
    blocks = [slice(r * rows, (r + 1) * rows) for r in range(n_blk)]
    ys = [jnp.dot(o_ref[blk, :], wo_ref[...], preferred_element_type=F32) for blk in blocks]
    for blk, y in zip(blocks, ys):
        _residual_and_next_norm(x_ref[blk, :], y, gate_ref[b, :], gpost_ref[...], gnext_ref[...],
                                sh_ref[b, :], sc_ref[b, :], x_out_ref.at[blk, :], h_out_ref.at[blk, :])


def _attn_post(o, x, w_o, mod, layer, g_post, g_next):
    bsz, s, d = x.shape
    tm = TM_POST
    tile = lambda: pl.BlockSpec((None, tm, d), lambda b, i: (b, i, 0))
    return pl.pallas_call(
        _attn_post_kernel,
        out_shape=(jax.ShapeDtypeStruct((bsz, s, d), F32), jax.ShapeDtypeStruct((bsz, s, d), BF16)),
        grid=(bsz, s // tm),
        in_specs=[tile(), tile(), _resident(w_o.shape),
                  _mod_spec(layer, 2), _mod_spec(layer, 3), _mod_spec(layer, 4),
                  _resident(g_post.shape), _resident(g_next.shape)],
        out_specs=(tile(), tile()),
        compiler_params=_cparams(2),
        name="attn_post",
    )(o, x, w_o, mod, mod, mod, g_post, g_next)


def _halo_specs(tm, d, s):
    per = tm // HALO
    last = s // HALO - 1
    main = pl.BlockSpec((None, tm, d), lambda b, i: (b, i, 0))
    prev = pl.BlockSpec((None, HALO, d), lambda b, i: (b, jnp.maximum(i * per - 1, 0), 0))
    nxt = pl.BlockSpec((None, HALO, d), lambda b, i: (b, jnp.minimum((i + 1) * per, last), 0))
    return prev, main, nxt


def _gather_halo(hp_ref, h_ref, hn_ref, hext_ref):
    i = pl.program_id(1)
    tm = h_ref.shape[0]
    hp = hp_ref[...]
    hn = hn_ref[...]
    has_prev = jnp.broadcast_to((i > 0).astype(jnp.int32), hp.shape) > 0
    has_next = jnp.broadcast_to((i < pl.num_programs(1) - 1).astype(jnp.int32), hn.shape) > 0
    hext_ref[0:HALO, :] = jnp.where(has_prev, hp, jnp.zeros_like(hp))
    hext_ref[HALO:HALO + tm, :] = h_ref[...]
    hext_ref[HALO + tm:, :] = jnp.where(has_next, hn, jnp.zeros_like(hn))


def _ffn_kernel(*refs, emit_next):
    (hp_ref, h_ref, hn_ref, x_ref, wup_ref, wdw_ref, bdw_ref, wdn_ref, gate_ref, gpost_ref) = refs[:10]
    if emit_next:
        gnext_ref, sh_ref, sc_ref, x_out_ref, h_out_ref, hext_ref, ubuf_ref = refs[10:]
    else:
        (x_out_ref, hext_ref, ubuf_ref), h_out_ref = refs[10:], None
    tm = h_ref.shape[0]
    n_chunks = FFN_DIM // FFN_CHUNK
    n_buf = ubuf_ref.shape[0]
    _gather_halo(hp_ref, h_ref, hn_ref, hext_ref)

    def up_project(c, u_ref):
        hext = hext_ref[...]
        cols = lambda blk: pl.ds(blk * FFN_CHUNK, FFN_CHUNK)
        u_ref[0] = jnp.dot(hext, wup_ref[:, cols(c)], preferred_element_type=F32)
        u_ref[1] = jnp.dot(hext, wup_ref[:, cols(n_chunks + c)], preferred_element_type=F32)

    def conv3(c, u_ref, j):
        w = wdw_ref[j * n_chunks + c]
        return (u_ref[j, pl.ds(HALO - 1, tm), :] * w[0:1]
                + u_ref[j, pl.ds(HALO, tm), :] * w[1:2]
                + u_ref[j, pl.ds(HALO + 1, tm), :] * w[2:3]) + bdw_ref[j * n_chunks + c]

    def down_project(c, u_ref):
        gate = conv3(c, u_ref, 0)
        val = conv3(c, u_ref, 1)
        z = (gate * jax.nn.sigmoid(gate) * val).astype(BF16)
        rows = pl.ds(c * FFN_CHUNK, FFN_CHUNK)
        return jnp.dot(z, wdn_ref[rows, :], preferred_element_type=F32)

    ahead = n_buf - 1
    for c in range(ahead):
        up_project(c, ubuf_ref.at[c])
    y = None
    for c in range(n_chunks):
        if c + ahead < n_chunks:
            up_project(c + ahead, ubuf_ref.at[(c + ahead) % n_buf])
        part = down_project(c, ubuf_ref.at[c % n_buf])
        y = part if y is None else y + part

    b = pl.ds(pl.program_id(0), 1)
    if emit_next:
        _residual_and_next_norm(x_ref[...], y, gate_ref[b, :], gpost_ref[...], gnext_ref[...],
                                sh_ref[b, :], sc_ref[b, :], x_out_ref, h_out_ref)
    else:
        _residual_and_next_norm(x_ref[...], y, gate_ref[b, :], gpost_ref[...], None, None, None,
                                x_out_ref, None)


def _conv_ffn(h, x, w_up, w_dw, b_dw, w_down, mod, layer, g_post, g_next=None):
    bsz, s, d = x.shape
    tm = TM_FFN
    emit_next = g_next is not None
    n_blocks = 2 * FFN_DIM // FFN_CHUNK
    w_dw = w_dw.reshape(FFN_CONV_WIDTH, n_blocks, FFN_CHUNK).transpose(1, 0, 2)
    b_dw = b_dw.reshape(n_blocks, 1, FFN_CHUNK)
    tile = lambda: pl.BlockSpec((None, tm, d), lambda b, i: (b, i, 0))
    slab = lambda w: pl.BlockSpec((None,) + w.shape[1:], lambda *_: (layer, 0, 0),
                                  pipeline_mode=pl.Buffered(1))
    in_specs = [*_halo_specs(tm, d, s), tile(), slab(w_up), _resident(w_dw.shape),
                _resident(b_dw.shape), slab(w_down), _mod_spec(layer, 5),
                _resident(g_post.shape)]
    args = [h, h, h, x, w_up, w_dw, b_dw, w_down, mod, g_post]
    out_shape = [jax.ShapeDtypeStruct((bsz, s, d), F32)]
    out_specs = [tile()]
    if emit_next:
        in_specs += [_resident(g_next.shape), _mod_spec(layer + 1, 0), _mod_spec(layer + 1, 1)]
        args += [g_next, mod, mod]
        out_shape.append(jax.ShapeDtypeStruct((bsz, s, d), BF16))
        out_specs.append(tile())
    return pl.pallas_call(
        functools.partial(_ffn_kernel, emit_next=emit_next),
        out_shape=tuple(out_shape),
        grid=(bsz, s // tm),
        in_specs=in_specs,
        out_specs=tuple(out_specs),
        scratch_shapes=[pltpu.VMEM((tm + 2 * HALO, d), BF16),
                        pltpu.VMEM((FFN_BUFFERS, 2, tm + 2 * HALO, FFN_CHUNK), F32)],
        compiler_params=_cparams(2),
        name="conv_ffn",
    )(*args)


def _conformer_kernel(hp_ref, h_ref, hn_ref, x_ref, wpw1_ref, bpw1_ref, wdw_ref, bdw_ref, gln_ref,
                      bln_ref, wpw2_ref, bpw2_ref, gate_ref, sh_ref, sc_ref, gpost_ref, gnext_ref,
                      x_out_ref, h_out_ref, hext_ref, glu_ref, conv_ref):
    i = pl.program_id(1)
    tm = h_ref.shape[0]
    rows = tm + 2 * HALO
    d = D_MODEL
    n_blk = d // LANES
    _gather_halo(hp_ref, h_ref, hn_ref, hext_ref)

    first_row = jnp.where(i > 0, 0, HALO)
    end_row = jnp.where(i < pl.num_programs(1) - 1, rows, HALO + tm)
    half = rows // 2
    row_blocks = [slice(0, half), slice(half, rows)]
    us = [jnp.dot(hext_ref[blk, :], wpw1_ref[...], preferred_element_type=F32) for blk in row_blocks]
    for blk, u in zip(row_blocks, us):
        u = u + bpw1_ref[...]
        glu = u[:, :d] * jax.nn.sigmoid(u[:, d:])
        r = lax.broadcasted_iota(jnp.int32, (half, 1), 0) + blk.start
        glu = jnp.where(jnp.logical_and(r >= first_row, r < end_row), glu, 0.0)
        for c in range(n_blk):
            glu_ref[c, blk, :] = glu[:, c * LANES:(c + 1) * LANES]

    pad = (CONV_WIDTH - 1) // 2

    def conv_block(c, carry):
        w = wdw_ref[c]
        for r0 in range(0, tm, CONF_ROWS):
            acc = jnp.broadcast_to(bdw_ref[c], (CONF_ROWS, LANES))
            for k in range(CONV_WIDTH):
                acc = acc + glu_ref[c, pl.ds(HALO - pad + k + r0, CONF_ROWS), :] * w[k:k + 1, :]
            conv_ref[c, r0:r0 + CONF_ROWS, :] = acc
        return carry

    lax.fori_loop(0, n_blk, conv_block, 0)

    n_out = 4
    out_blocks = [slice(r * tm // n_out, (r + 1) * tm // n_out) for r in range(n_out)]
    ys = []
    for blk in out_blocks:
        v = jnp.concatenate([conv_ref[c, blk, :] for c in range(n_blk)], axis=-1)
        mu = jnp.mean(v, axis=-1, keepdims=True)
        var = jnp.mean(jnp.square(v - mu), axis=-1, keepdims=True)
        ln = (v - mu) * lax.rsqrt(var + EPS) * gln_ref[...] + bln_ref[...]
        z = (ln * jax.nn.sigmoid(ln)).astype(BF16)
        ys.append(jnp.dot(z, wpw2_ref[...], preferred_element_type=F32))

    b = pl.ds(pl.program_id(0), 1)
    for blk, y in zip(out_blocks, ys):
        _residual_and_next_norm(x_ref[blk, :], y + bpw2_ref[...], gate_ref[b, :], gpost_ref[...],
                                gnext_ref[...], sh_ref[b, :], sc_ref[b, :],
                                x_out_ref.at[blk, :], h_out_ref.at[blk, :])


def _conformer(h, x, w_pw1, b_pw1, w_dw_blk, b_dw_blk, g_ln, b_ln, w_pw2, b_pw2, mod, layer,
               g_post, g_next):
    bsz, s, d = x.shape
    tm = TM_CONF
    n_blk = d // LANES
    tile = lambda: pl.BlockSpec((None, tm, d), lambda b, i: (b, i, 0))
    consts = [w_pw1, b_pw1, w_dw_blk, b_dw_blk, g_ln, b_ln, w_pw2, b_pw2]
    return pl.pallas_call(
        _conformer_kernel,
        out_shape=(jax.ShapeDtypeStruct((bsz, s, d), F32), jax.ShapeDtypeStruct((bsz, s, d), BF16)),
        grid=(bsz, s // tm),
        in_specs=[*_halo_specs(tm, d, s), tile(), *[_resident(a.shape) for a in consts],
                  _mod_spec(layer, 2), _mod_spec(layer, 3), _mod_spec(layer, 4),
                  _resident(g_post.shape), _resident(g_next.shape)],
        out_specs=(tile(), tile()),
        scratch_shapes=[pltpu.VMEM((tm + 2 * HALO, d), BF16),
                        pltpu.VMEM((n_blk, tm + 2 * HALO, LANES), F32),
                        pltpu.VMEM((n_blk, tm, LANES), F32)],
        compiler_params=_cparams(2),
        name="conformer",
    )(h, h, h, x, *consts, mod, mod, mod, g_post, g_next)


def _rope_tables(seq_len):
    n = ROPE_DIM // 4
    pos = np.arange(seq_len)
    inv = np.float32(ROPE_THETA) ** (-np.arange(n, dtype=np.float32) / np.float32(n))
    ang_r = (pos // GRID_W).astype(np.float32)[:, None] * inv
    ang_c = (pos % GRID_W).astype(np.float32)[:, None] * inv
    zeros = np.zeros((seq_len, LANES - ROPE_DIM), np.float32)
    cos = np.concatenate([np.cos(ang_r), np.cos(ang_r), np.cos(ang_c), np.cos(ang_c), zeros], -1)
    sin = np.concatenate([-np.sin(ang_r), np.sin(ang_r), -np.sin(ang_c), np.sin(ang_c), zeros], -1)
    return jnp.asarray(cos, F32), jnp.asarray(sin, F32)


def _swap_halves(w):
    n = ROPE_DIM // 4
    return jnp.concatenate([w[..., n:2 * n], w[..., :n], w[..., 3 * n:], w[..., 2 * n:3 * n]], -1)


def _pad_lanes(w):
    return jnp.pad(w, [(0, 0)] * (w.ndim - 1) + [(0, LANES - w.shape[-1])])


def _mla_weights(w_dqkv, w_uq):
    kr = w_dqkv[:, Q_LORA + KV_LORA:]
    wd = jnp.concatenate([w_dqkv[:, :Q_LORA + KV_LORA], _pad_lanes(kr), _pad_lanes(_swap_halves(kr))], -1)
    wq = w_uq.reshape(Q_LORA, MLA_HEADS, QK_DIM)
    rope = wq[:, :, NOPE_DIM:]
    wq = jnp.concatenate([wq[:, :, :NOPE_DIM].reshape(Q_LORA, -1),
                          _pad_lanes(rope).reshape(Q_LORA, -1),
                          _pad_lanes(_swap_halves(rope)).reshape(Q_LORA, -1)], -1)
    return wd.astype(BF16), wq.astype(BF16)


def kernel(x, c, ctx, c_ctx, w_mod, b_mod, g_pre_mix, g_post_mix, g_pre_ffn, g_post_ffn, mla_w_dqkv, mla_g_q, mla_w_uq, mla_g_kv, mla_w_ukv, mla_w_o, cv_w_pw1, cv_b_pw1, cv_w_dw, cv_b_dw, cv_g_ln, cv_b_ln, cv_w_pw2, cv_b_pw2, ffn_w_up, ffn_w_dw, ffn_b_dw, ffn_w_down):
    bsz, s, d = x.shape
    assert (bsz, d) == (c.shape[0], D_MODEL) and bsz + 1 <= MOD_ROWS and DEPTH == 2
    row = lambda v: v.reshape(1, -1)

    cc = jnp.concatenate([c, c_ctx[None, :], jnp.zeros((MOD_ROWS - bsz - 1, d), F32)], axis=0)
    mod = _adaln_mod(cc, w_mod, b_mod)

    cos, sin = _rope_tables(s)
    wd, wq = _mla_weights(mla_w_dqkv[0], mla_w_uq[0])
    wukv = mla_w_ukv[0].astype(BF16)
    g0 = row(g_pre_mix[0])
    g_kv = row(mla_g_kv[0])
    q, k_lat, v_lat = _mla_project_latent(x, mod, g0, wd, row(mla_g_q[0]), wq, g_kv, wukv, cos, sin)
    k_ctx, v_ctx = _mla_project_context(ctx, mod, g0, wd[:, Q_LORA:Q_LORA + KV_LORA + LANES], g_kv,
                                        wukv, bsz)
    o = _mla_attention(q, k_ctx, k_lat, v_ctx, v_lat)
    x1, h = _attn_post(o, x, mla_w_o[0].astype(BF16), mod, 0, row(g_post_mix[0]), row(g_pre_ffn[0]))
    w_up_all = ffn_w_up.astype(BF16)
    w_down_all = ffn_w_down.astype(BF16)
    x2, h = _conv_ffn(h, x1, w_up_all, ffn_w_dw[0], ffn_b_dw[0], w_down_all, mod, 0,
                      row(g_post_ffn[0]), row(g_pre_mix[1]))

    n_blk = d // LANES
    w_dw_blk = jnp.pad(cv_w_dw[0], ((0, 1), (0, 0))).reshape(CONV_WIDTH + 1, n_blk, LANES)
    w_dw_blk = w_dw_blk.transpose(1, 0, 2)
    b_dw_blk = cv_b_dw[0].reshape(n_blk, 1, LANES)
    x3, h = _conformer(h, x2, cv_w_pw1[0].astype(BF16), row(cv_b_pw1[0]), w_dw_blk, b_dw_blk,
                       row(cv_g_ln[0]), row(cv_b_ln[0]), cv_w_pw2[0].astype(BF16), row(cv_b_pw2[0]),
                       mod, 1, row(g_post_mix[1]), row(g_pre_ffn[1]))
    (x4,) = _conv_ffn(h, x3, w_up_all, ffn_w_dw[1], ffn_b_dw[1], w_down_all, mod, 1,
                      row(g_post_ffn[1]))
    return x4
```
